```python
import math
import jax, jax.numpy as jnp
from jax import lax
import numpy as np

D_MODEL = 4096
BATCH = 4
SEQ = 2048
DEPTH = 2

CHUNK = 64
Q_BLOCK = 128
MIX_WIDTH = D_MODEL
N_GROUPS = 4
GROUP_WIDTH = MIX_WIDTH // N_GROUPS
HEAD_DIM = 128
N_HEADS = GROUP_WIDTH // HEAD_DIM
DIFF_HALF = HEAD_DIM // 2
GDN_CONV = 4
MLA_Q_RANK = 3 * D_MODEL // 16
MLA_KV_RANK = D_MODEL // 16
MLA_NOPE = 128
MLA_ROPE = 64
MLA_V = HEAD_DIM
ROPE_THETA = 10000.0
BAND_CHUNKS = 9
BAND = BAND_CHUNKS * CHUNK
REL_CLIP = 128
DEEPNORM_ALPHA = (2 * DEPTH) ** 0.25
DEEPNORM_BETA = (8 * DEPTH) ** -0.25
IN_SIZES = (GROUP_WIDTH, GROUP_WIDTH, GROUP_WIDTH,
            3 * GROUP_WIDTH, N_HEADS, N_HEADS,
            MLA_Q_RANK, MLA_KV_RANK + MLA_ROPE,
            GROUP_WIDTH, GROUP_WIDTH, GROUP_WIDTH,
            MIX_WIDTH)
IN_COLS = sum(IN_SIZES)

kernel_name = 'hymba_style_streaming_hybrid_encoder'


def _rms_norm(x, g, eps=1e-6):
    xf = x.astype(jnp.float32)
    y = xf * lax.rsqrt(jnp.mean(xf * xf, axis=-1, keepdims=True) + eps)
    return (y * g.astype(jnp.float32)).astype(x.dtype)


def _layer_norm(x, g, b, eps=1e-5):
    xf = x.astype(jnp.float32)
    xc = xf - jnp.mean(xf, axis=-1, keepdims=True)
    var = jnp.mean(xc * xc, axis=-1, keepdims=True)
    return (xc * lax.rsqrt(var + eps) * g.astype(jnp.float32) + b.astype(jnp.float32)).astype(x.dtype)


def _l2_normalize(x, eps=1e-6):
    return x * lax.rsqrt(jnp.sum(x * x, axis=-1, keepdims=True) + eps)


def _split_cols(h, sizes):
    out, start = [], 0
    for n in sizes:
        out.append(h[..., start:start + n])
        start += n
    return out


def _chunk_visible(qpos, kpos):
    return (kpos[None, :] // CHUNK) <= (qpos[:, None] // CHUNK)


def _sweep_query_blocks(q, block_fn):
    b, t = q.shape[:2]
    nqb = t // Q_BLOCK
    qb = jnp.moveaxis(q.reshape((b, nqb, Q_BLOCK) + q.shape[2:]), 1, 0)
    out = lax.map(lambda a: block_fn(a[0], a[1] * Q_BLOCK + jnp.arange(Q_BLOCK)), (qb, jnp.arange(nqb)))
    out = jnp.moveaxis(out, 0, 1)
    return out.reshape((b, t) + out.shape[3:])


def _diff_attention(q, k, v, lam):
    b, t = q.shape[:2]
    kpos = jnp.arange(t)
    slopes = 2.0 ** (-8.0 * jnp.arange(1, N_HEADS + 1, dtype=jnp.float32) / N_HEADS)
    scale = DIFF_HALF ** -0.5

    def block(qi, qpos):
        s = jnp.einsum('bqmd,bkmd->bmqk', qi, k).astype(jnp.float32) * scale
        s = s.reshape(b, N_HEADS, 2, Q_BLOCK, t)
        dist = jnp.abs(qpos[:, None] - kpos[None, :]).astype(jnp.float32)
        s = s - slopes[:, None, None, None] * dist
        s = jnp.where(_chunk_visible(qpos, kpos), s, -jnp.inf)
        p = jax.nn.softmax(s, axis=-1)
        w = p[:, :, 0] - lam * p[:, :, 1]
        return jnp.einsum('bhqk,bkhd->bqhd', w.astype(v.dtype), v)

    return _sweep_query_blocks(q, block)


def _causal_depthwise_conv(x, w):
    kw = w.shape[0]
    return lax.conv_general_dilated(x, w[:, None, :], window_strides=(1,), padding=[(kw - 1, 0)],
                                    dimension_numbers=('NWC', 'WIO', 'NWC'),
                                    feature_group_count=x.shape[-1])


def _gated_delta_net(qkv, a, bt, conv_w, a_log, dt_bias, norm_w):
    out_dtype = qkv.dtype
    f32 = jnp.float32
    b, t, _ = qkv.shape
    nc = t // CHUNK
    qkv = jax.nn.silu(_causal_depthwise_conv(qkv, conv_w)).astype(f32)
    q, k, v = jnp.split(qkv, 3, axis=-1)

    def to_chunks(z):
        z = z.reshape((b, nc, CHUNK) + z.shape[2:])
        return jnp.moveaxis(z, 3, 1)

    q = to_chunks(_l2_normalize(q.reshape(b, t, N_HEADS, HEAD_DIM)) * HEAD_DIM ** -0.5)
    k = to_chunks(_l2_normalize(k.reshape(b, t, N_HEADS, HEAD_DIM)))
    v = to_chunks(v.reshape(b, t, N_HEADS, HEAD_DIM))
    beta = to_chunks(jax.nn.sigmoid(bt.astype(f32)))
    g = -jnp.exp(a_log.astype(f32)) * jax.nn.softplus(a.astype(f32) + dt_bias.astype(f32))
    g = jnp.cumsum(to_chunks(g), axis=-1)

    lower = jnp.tril(jnp.ones((CHUNK, CHUNK), dtype=bool))
    strict = jnp.tril(jnp.ones((CHUNK, CHUNK), dtype=bool), -1)
    gdiff = g[..., :, None] - g[..., None, :]
    decay = jnp.where(lower, jnp.exp(jnp.where(lower, gdiff, 0.0)), 0.0)
    k_beta = k * beta[..., None]
    kk = jnp.einsum('bhncd,bhnsd->bhncs', k_beta, k) * decay
    tri = jnp.where(strict, kk, 0.0) + jnp.eye(CHUNK, dtype=f32)
    u = lax.linalg.triangular_solve(tri, v * beta[..., None], left_side=True, lower=True, unit_diagonal=True)
    w = lax.linalg.triangular_solve(tri, k_beta * jnp.exp(g)[..., None], left_side=True, lower=True,
                                    unit_diagonal=True)
    qk = jnp.where(lower, jnp.einsum('bhncd,bhnsd->bhncs', q, k) * decay, 0.0)

    def step(state, inp):
        q_c, k_c, u_c, w_c, g_c, qk_c = inp
        v_new = u_c - jnp.einsum('bhck,bhkv->bhcv', w_c, state)
        o_c = (jnp.einsum('bhck,bhkv->bhcv', q_c * jnp.exp(g_c)[..., None], state)
               + jnp.einsum('bhcs,bhsv->bhcv', qk_c, v_new))
        g_last = g_c[..., -1:]
        state = (state * jnp.exp(g_last)[..., None]
                 + jnp.einsum('bhck,bhcv->bhkv', k_c * jnp.exp(g_last - g_c)[..., None], v_new))
        return state, o_c

    xs = [jnp.moveaxis(z, 2, 0) for z in (q, k, u, w, g, qk)]
    state0 = jnp.zeros((b, N_HEADS, HEAD_DIM, HEAD_DIM), f32)
    _, o = lax.scan(step, state0, xs)
    o = jnp.moveaxis(jnp.moveaxis(o, 0, 2), 1, 3).reshape(b, t, N_HEADS, HEAD_DIM)
    return _rms_norm(o, norm_w).astype(out_dtype)


def _rope(x, pos):
    half = MLA_ROPE // 2
    inv = ROPE_THETA ** (-jnp.arange(half, dtype=jnp.float32) / half)
    ang = pos.astype(jnp.float32)[:, None] * inv[None, :]
    cos = jnp.cos(ang)[None, :, None, :]
    sin = jnp.sin(ang)[None, :, None, :]
    xf = x.astype(jnp.float32)
    x1, x2 = xf[..., :half], xf[..., half:]
    return jnp.concatenate([x1 * cos - x2 * sin, x2 * cos + x1 * sin], axis=-1).astype(x.dtype)


def _mla(c_dq, c_dkv, q_norm, w_uq, kv_norm, w_ukv):
    b, t = c_dq.shape[:2]
    pos = jnp.arange(t)
    q = (_rms_norm(c_dq, q_norm) @ w_uq).reshape(b, t, N_HEADS, MLA_NOPE + MLA_ROPE)
    q = jnp.concatenate([q[..., :MLA_NOPE], _rope(q[..., MLA_NOPE:], pos)], axis=-1)
    kv = (_rms_norm(c_dkv[..., :MLA_KV_RANK], kv_norm) @ w_ukv).reshape(b, t, N_HEADS, MLA_NOPE + MLA_V)
    k_rope = _rope(c_dkv[..., None, MLA_KV_RANK:], pos)
    k = jnp.concatenate([kv[..., :MLA_NOPE], jnp.broadcast_to(k_rope, (b, t, N_HEADS, MLA_ROPE))], axis=-1)
    v = kv[..., MLA_NOPE:]
    kpos = jnp.arange(t)
    scale = (MLA_NOPE + MLA_ROPE) ** -0.5

    def block(qi, qpos):
        s = jnp.einsum('bqhd,bkhd->bhqk', qi, k).astype(jnp.float32) * scale
        s = jnp.where(_chunk_visible(qpos, kpos), s, -jnp.inf)
        p = jax.nn.softmax(s, axis=-1)
        return jnp.einsum('bhqk,bkhd->bqhd', p.astype(v.dtype), v)

    return _sweep_query_blocks(q, block)


def _band_attention(q, k, v, rel_bias):
    b, t = q.shape[:2]
    nc = t // CHUNK
    pad = (BAND_CHUNKS - 1) * CHUNK
    kp = jnp.pad(k, ((0, 0), (pad, 0), (0, 0), (0, 0)))
    vp = jnp.pad(v, ((0, 0), (pad, 0), (0, 0), (0, 0)))
    band_idx = jnp.arange(BAND)
    rel = jnp.arange(CHUNK)[:, None] + pad - band_idx[None, :]
    bias = rel_bias.astype(jnp.float32)[:, jnp.clip(rel, -REL_CLIP, REL_CLIP) + REL_CLIP]
    qc = jnp.moveaxis(q.reshape(b, nc, CHUNK, N_HEADS, HEAD_DIM), 1, 0)
    scale = HEAD_DIM ** -0.5

    def chunk_fn(args):
        qi, c = args
        start = c * CHUNK
        kb = lax.dynamic_slice_in_dim(kp, start, BAND, axis=1)
        vb = lax.dynamic_slice_in_dim(vp, start, BAND, axis=1)
        s = jnp.einsum('bqhd,bkhd->bhqk', qi, kb).astype(jnp.float32) * scale + bias
        s = jnp.where(start + band_idx >= pad, s, -jnp.inf)
        p = jax.nn.softmax(s, axis=-1)
        return jnp.einsum('bhqk,bkhd->bqhd', p.astype(vb.dtype), vb)

    out = lax.map(chunk_fn, (qc, jnp.arange(nc)))
    return jnp.moveaxis(out, 0, 1).reshape(b, t, N_HEADS, HEAD_DIM)


def _hybrid_layer(x, layer_idx, w_in, diff_lambda, diff_norm, gdn_conv, gdn_a_log, gdn_dt_bias, gdn_norm,
                  mla_q_norm, mla_w_uq, mla_kv_norm, mla_w_ukv, rel_bias, w_out, ln_gain, ln_bias):
    b, t, _ = x.shape
    h = jnp.einsum('btd,dc->btc', x, w_in)
    (a_q, a_k, a_v, b_qkv, b_a, b_b, c_dq, c_dkv, d_q, d_k, d_v, gate) = _split_cols(h, IN_SIZES)

    lam_init = 0.8 - 0.6 * math.exp(-0.3 * layer_idx)
    lam_p = diff_lambda.astype(jnp.float32)
    lam = jnp.exp(jnp.sum(lam_p[0] * lam_p[1])) - jnp.exp(jnp.sum(lam_p[2] * lam_p[3])) + lam_init
    o_a = _diff_attention(a_q.reshape(b, t, 2 * N_HEADS, DIFF_HALF), a_k.reshape(b, t, 2 * N_HEADS, DIFF_HALF),
                          a_v.reshape(b, t, N_HEADS, HEAD_DIM), lam)
    o_a = _rms_norm(o_a, diff_norm) * (1.0 - lam_init)

    o_b = _gated_delta_net(b_qkv, b_a, b_b, gdn_conv, gdn_a_log, gdn_dt_bias, gdn_norm)

    o_c = _mla(c_dq, c_dkv, mla_q_norm, mla_w_uq, mla_kv_norm, mla_w_ukv)

    o_d = _band_attention(d_q.reshape(b, t, N_HEADS, HEAD_DIM), d_k.reshape(b, t, N_HEADS, HEAD_DIM),
                          d_v.reshape(b, t, N_HEADS, HEAD_DIM), rel_bias)

    o = jnp.concatenate([o_a.reshape(b, t, GROUP_WIDTH), o_b.reshape(b, t, GROUP_WIDTH),
                         o_c.reshape(b, t, GROUP_WIDTH), o_d.reshape(b, t, GROUP_WIDTH)], axis=-1)
    o = o * jax.nn.silu(gate)
    y = jnp.einsum('btm,md->btd', o, w_out)
    return _layer_norm(DEEPNORM_ALPHA * x + y, ln_gain, ln_bias)


def setup_inputs(seed: int = 0) -> dict:
    key = jax.random.key(seed)
    ks = jax.random.split(key, 16)
    f32 = jnp.float32
    x = jax.random.normal(ks[0], (BATCH, SEQ, D_MODEL), f32)
    ones = lambda n: jnp.ones((n,), f32)
    vals = lambda n: jnp.full((n,), DEEPNORM_BETA, f32)
    col_scale = jnp.concatenate([ones(2 * GROUP_WIDTH), vals(GROUP_WIDTH),
                                 ones(2 * GROUP_WIDTH), vals(GROUP_WIDTH),
                                 ones(2 * N_HEADS + MLA_Q_RANK + MLA_KV_RANK + MLA_ROPE + 2 * GROUP_WIDTH),
                                 vals(GROUP_WIDTH), ones(MIX_WIDTH)])
    w_in = jax.random.normal(ks[1], (DEPTH, D_MODEL, IN_COLS), f32) * (D_MODEL ** -0.5) * col_scale
    diff_lambda = 0.1 * jax.random.normal(ks[2], (DEPTH, 4, DIFF_HALF), f32)
    diff_norm = 1.0 + 0.02 * jax.random.normal(ks[3], (DEPTH, HEAD_DIM), f32)
    gdn_conv = jax.random.normal(ks[4], (DEPTH, GDN_CONV, 3 * GROUP_WIDTH), f32) * (GDN_CONV ** -0.5)
    gdn_a_log = jnp.log(jax.random.uniform(ks[5], (DEPTH, N_HEADS), f32, 1.0, 16.0))
    dt = jnp.exp(jax.random.uniform(ks[6], (DEPTH, N_HEADS), f32, math.log(1e-3), math.log(1e-1)))
    gdn_dt_bias = dt + jnp.log(-jnp.expm1(-dt))
    gdn_norm = 1.0 + 0.02 * jax.random.normal(ks[7], (DEPTH, HEAD_DIM), f32)
    mla_q_norm = 1.0 + 0.02 * jax.random.normal(ks[8], (DEPTH, MLA_Q_RANK), f32)
    mla_w_uq = jax.random.normal(ks[9], (DEPTH, MLA_Q_RANK, N_HEADS * (MLA_NOPE + MLA_ROPE)), f32) * (MLA_Q_RANK ** -0.5)
    mla_kv_norm = 1.0 + 0.02 * jax.random.normal(ks[10], (DEPTH, MLA_KV_RANK), f32)
    ukv_scale = jnp.tile(jnp.concatenate([ones(MLA_NOPE), vals(MLA_V)]), N_HEADS)
    mla_w_ukv = (jax.random.normal(ks[11], (DEPTH, MLA_KV_RANK, N_HEADS * (MLA_NOPE + MLA_V)), f32)
                 * (MLA_KV_RANK ** -0.5) * ukv_scale)
    rel_bias = 0.5 * jax.random.normal(ks[12], (DEPTH, N_HEADS, 2 * REL_CLIP + 1), f32)
    w_out = jax.random.normal(ks[13], (DEPTH, MIX_WIDTH, D_MODEL), f32) * (MIX_WIDTH ** -0.5) * DEEPNORM_BETA
    ln_gain = 1.0 + 0.02 * jax.random.normal(ks[14], (DEPTH, D_MODEL), f32)
    ln_bias = 0.02 * jax.random.normal(ks[15], (DEPTH, D_MODEL), f32)
    return {'x': x, 'w_in': w_in, 'diff_lambda': diff_lambda, 'diff_norm': diff_norm,
            'gdn_conv': gdn_conv, 'gdn_a_log': gdn_a_log, 'gdn_dt_bias': gdn_dt_bias, 'gdn_norm': gdn_norm,
            'mla_q_norm': mla_q_norm, 'mla_w_uq': mla_w_uq, 'mla_kv_norm': mla_kv_norm, 'mla_w_ukv': mla_w_ukv,
            'rel_bias': rel_bias, 'w_out': w_out, 'ln_gain': ln_gain, 'ln_bias': ln_bias}


def reference(x, w_in, diff_lambda, diff_norm, gdn_conv, gdn_a_log, gdn_dt_bias, gdn_norm,
              mla_q_norm, mla_w_uq, mla_kv_norm, mla_w_ukv, rel_bias, w_out, ln_gain, ln_bias):
    for l in range(DEPTH):
        x = _hybrid_layer(x, l, w_in[l], diff_lambda[l], diff_norm[l], gdn_conv[l], gdn_a_log[l],
                          gdn_dt_bias[l], gdn_norm[l], mla_q_norm[l], mla_w_uq[l], mla_kv_norm[l],
                          mla_w_ukv[l], rel_bias[l], w_out[l], ln_gain[l], ln_bias[l])
    return x
```

```python
import functools
import math

import jax
import jax.numpy as jnp
from jax import lax
from jax.experimental import pallas as pl
from jax.experimental.pallas import tpu as pltpu

F32 = jnp.float32
BF16 = jnp.bfloat16

D_MODEL = 4096
DEPTH = 2
CHUNK = 64
N_HEADS = 8
HEAD_DIM = 128
GROUP_WIDTH = N_HEADS * HEAD_DIM
DIFF_HALF = HEAD_DIM // 2
GDN_CONV = 4
MLA_Q_RANK = 768
MLA_KV_RANK = 256
MLA_NOPE = 128
MLA_ROPE = 64
ROPE_THETA = 10000.0
BAND_CHUNKS = 9
REL_CLIP = 128
DEEPNORM_ALPHA = (2 * DEPTH) ** 0.25
IN_SIZES = (GROUP_WIDTH, GROUP_WIDTH, GROUP_WIDTH, 3 * GROUP_WIDTH, N_HEADS, N_HEADS,
            MLA_Q_RANK, MLA_KV_RANK + MLA_ROPE, GROUP_WIDTH, GROUP_WIDTH, GROUP_WIDTH, D_MODEL)

LANE = 128
COL_GDN_QKV = 0
COL_MLA_Q = 24
COL_MLA_KV = 30
COL_GDN_AB = 33
COL_DIFF_Q = 34
COL_DIFF_K = 42
COL_DIFF_V = 50
COL_BAND_Q = 58
COL_BAND_K = 66
COL_BAND_V = 74
COL_GATE = 82
N_COL_BLOCKS = 114
N_COLS = N_COL_BLOCKS * LANE

ATT_BLOCK = 128
GDN_GROUP = 256
NEG_BIG = -1e30
VMEM_LIMIT = 48 * 1024 * 1024


def _params(*sem):
    return pltpu.CompilerParams(dimension_semantics=sem, vmem_limit_bytes=VMEM_LIMIT)


def _dot(a, b):
    return jnp.dot(a, b, preferred_element_type=F32)


def _dot_nt(a, b):
    return lax.dot_general(a, b, (((1,), (1,)), ((), ())), preferred_element_type=F32)


def _silu(x):
    return x * (1.0 / (1.0 + jnp.exp(-x)))


def _matmul_kernel(x_ref, w_ref, o_ref):
    o_ref[...] = _dot(x_ref[...], w_ref[...]).astype(o_ref.dtype)


def _in_proj(xb, wb):
    m = xb.shape[0]
    tm, tn = 512, 768
    return pl.pallas_call(
        _matmul_kernel,
        grid=(N_COLS // tn, m // tm),
        in_specs=[pl.BlockSpec((tm, D_MODEL), lambda j, i: (i, 0)),
                  pl.BlockSpec((D_MODEL, tn), lambda j, i: (0, j))],
        out_specs=pl.BlockSpec((tm, tn), lambda j, i: (i, j)),
        out_shape=jax.ShapeDtypeStruct((m, N_COLS), BF16),
        compiler_params=_params("arbitrary", "arbitrary"),
        name="in_proj",
    )(xb, wb)


def _flash_loop(q, k_ref, v_ref, lo, hi, score_fn, n_maps=1):
    tq = q[0].shape[0]

    def body(kb, carry):
        start = pl.multiple_of(kb * ATT_BLOCK, ATT_BLOCK)
        k = k_ref[pl.ds(start, ATT_BLOCK), :]
        v = v_ref[pl.ds(start, ATT_BLOCK), :]
        new = []
        for mi in range(n_maps):
            m, l, acc = carry[mi]
            s = score_fn(_dot_nt(q[mi], k), kb, mi)
            m_new = jnp.maximum(m, jnp.max(s, axis=-1, keepdims=True))
            alpha = jnp.exp(m - m_new)
            p = jnp.exp(s - m_new)
            l = alpha * l + jnp.sum(p, axis=-1, keepdims=True)
            acc = alpha * acc + _dot(p.astype(BF16), v)
            new.append((m_new, l, acc))
        return tuple(new)

    init = tuple((jnp.full((tq, 1), NEG_BIG, F32), jnp.zeros((tq, 1), F32),
                  jnp.zeros((tq, HEAD_DIM), F32)) for _ in range(n_maps))
    out = lax.fori_loop(lo, hi + 1, body, init)
    return [(acc, l) for (_, l, acc) in out]


def _block_iotas():
    r = lax.broadcasted_iota(jnp.int32, (ATT_BLOCK, ATT_BLOCK), 0)
    c = lax.broadcasted_iota(jnp.int32, (ATT_BLOCK, ATT_BLOCK), 1)
    return r, c


def _diff_attn_kernel(lam_init, q_ref, k_ref, v_ref, gate_ref, slope_ref, lam_ref, nw_ref, o_ref):
    i = pl.program_id(2)
    q = q_ref[...]
    lane = lax.broadcasted_iota(jnp.int32, q.shape, 1)
    q1 = jnp.where(lane < DIFF_HALF, q, jnp.zeros_like(q))
    q2 = jnp.where(lane >= DIFF_HALF, q, jnp.zeros_like(q))
    r, c = _block_iotas()
    rc = (r - c).astype(F32)
    diag_visible = (c // CHUNK) <= (r // CHUNK)
    slope = slope_ref[:, 0:1]
    scale = DIFF_HALF ** -0.5

    def score_fn(s, kb, mi):
        dist = jnp.abs(rc + ((i - kb) * ATT_BLOCK).astype(F32))
        s = s * scale - slope * dist
        return jnp.where(jnp.logical_or(diag_visible, kb < i), s, NEG_BIG)

    (acc1, l1), (acc2, l2) = _flash_loop((q1, q2), k_ref, v_ref, 0, i, score_fn, n_maps=2)
    lp = lam_ref[...]
    lam = (jnp.exp(jnp.sum(lp[0:1] * lp[1:2], axis=-1, keepdims=True))
           - jnp.exp(jnp.sum(lp[2:3] * lp[3:4], axis=-1, keepdims=True)) + lam_init)
    o = acc1 / l1 - lam * (acc2 / l2)
    o = o * lax.rsqrt(jnp.mean(o * o, axis=-1, keepdims=True) + 1e-6) * nw_ref[...]
    o = o * (1.0 - lam_init)
    o_ref[...] = (o * _silu(gate_ref[...].astype(F32))).astype(o_ref.dtype)


def _diff_attention(h2d, slopes, lam_p, norm_w, lam_init, b, t):
    nq = t // ATT_BLOCK
    m = b * t
    blk = (ATT_BLOCK, HEAD_DIM)
    return pl.pallas_call(
        functools.partial(_diff_attn_kernel, lam_init),
        grid=(b, N_HEADS, nq),
        in_specs=[pl.BlockSpec(blk, lambda bi, h, i: (bi * nq + i, COL_DIFF_Q + h)),
                  pl.BlockSpec((t, HEAD_DIM), lambda bi, h, i: (bi, COL_DIFF_K + h)),
                  pl.BlockSpec((t, HEAD_DIM), lambda bi, h, i: (bi, COL_DIFF_V + h)),
                  pl.BlockSpec(blk, lambda bi, h, i: (bi * nq + i, COL_GATE + h)),
                  pl.BlockSpec((None, 1, LANE), lambda bi, h, i: (h, 0, 0)),
                  pl.BlockSpec((4, LANE), lambda bi, h, i: (0, 0)),
                  pl.BlockSpec((1, HEAD_DIM), lambda bi, h, i: (0, 0))],
        out_specs=pl.BlockSpec(blk, lambda bi, h, i: (bi * nq + i, h)),
        out_shape=jax.ShapeDtypeStruct((m, GROUP_WIDTH), BF16),
        compiler_params=_params("arbitrary", "arbitrary", "arbitrary"),
        name="diff_attention",
    )(h2d, h2d, h2d, h2d, slopes, lam_p, norm_w)


def _mla_attn_kernel(q_ref, k_ref, v_ref, gate_ref, o_ref):
    i = pl.program_id(2)
    r, c = _block_iotas()
    diag_visible = (c // CHUNK) <= (r // CHUNK)
    scale = (MLA_NOPE + MLA_ROPE) ** -0.5

    def score_fn(s, kb, mi):
        return jnp.where(jnp.logical_or(diag_visible, kb < i), s * scale, NEG_BIG)

    ((acc, l),) = _flash_loop((q_ref[...],), k_ref, v_ref, 0, i, score_fn)
    o_ref[...] = ((acc / l) * _silu(gate_ref[...].astype(F32))).astype(o_ref.dtype)


def _mla_attention(qf, kf, vv, h2d, b, t):
    nq = t // ATT_BLOCK
    m = b * t
    return pl.pallas_call(
        _mla_attn_kernel,
        grid=(b, N_HEADS, nq),
        in_specs=[pl.BlockSpec((ATT_BLOCK, 2 * LANE), lambda bi, h, i: (bi * nq + i, h)),
                  pl.BlockSpec((t, 2 * LANE), lambda bi, h, i: (bi, h)),
                  pl.BlockSpec((t, HEAD_DIM), lambda bi, h, i: (bi, h)),
                  pl.BlockSpec((ATT_BLOCK, HEAD_DIM), lambda bi, h, i: (bi * nq + i, COL_GATE + 16 + h))],
        out_specs=pl.BlockSpec((ATT_BLOCK, HEAD_DIM), lambda bi, h, i: (bi * nq + i, h)),
        out_shape=jax.ShapeDtypeStruct((m, GROUP_WIDTH), BF16),
        compiler_params=_params("arbitrary", "arbitrary", "arbitrary"),
        name="mla_attention",
    )(qf, kf, vv, h2d)


BAND_KEY_BLOCKS = (BAND_CHUNKS - 1) * CHUNK // ATT_BLOCK + 1


def _band_attn_kernel(q_ref, k_ref, v_ref, gate_ref, bias_ref, o_ref):
    i = pl.program_id(2)
    scale = HEAD_DIM ** -0.5

    def score_fn(s, kb, mi):
        return s * scale + bias_ref[kb - i + (BAND_KEY_BLOCKS - 1)]

    lo = jnp.maximum(i - (BAND_KEY_BLOCKS - 1), 0)
    ((acc, l),) = _flash_loop((q_ref[...],), k_ref, v_ref, lo, i, score_fn)
    o_ref[...] = ((acc / l) * _silu(gate_ref[...].astype(F32))).astype(o_ref.dtype)


def _band_attention(h2d, bias_tiles, b, t):
    nq = t // ATT_BLOCK
    m = b * t
    blk = (ATT_BLOCK, HEAD_DIM)
    return pl.pallas_call(
        _band_attn_kernel,
        grid=(b, N_HEADS, nq),
        in_specs=[pl.BlockSpec(blk, lambda bi, h, i: (bi * nq + i, COL_BAND_Q + h)),
                  pl.BlockSpec((t, HEAD_DIM), lambda bi, h, i: (bi, COL_BAND_K + h)),
                  pl.BlockSpec((t, HEAD_DIM), lambda bi, h, i: (bi, COL_BAND_V + h)),
                  pl.BlockSpec(blk, lambda bi, h, i: (bi * nq + i, COL_GATE + 24 + h)),
                  pl.BlockSpec((None, BAND_KEY_BLOCKS, ATT_BLOCK, ATT_BLOCK), lambda bi, h, i: (h, 0, 0, 0))],
        out_specs=pl.BlockSpec(blk, lambda bi, h, i: (bi * nq + i, h)),
        out_shape=jax.ShapeDtypeStruct((m, GROUP_WIDTH), BF16),
        compiler_params=_params("arbitrary", "arbitrary", "arbitrary"),
        name="band_attention",
    )(h2d, h2d, h2d, h2d, bias_tiles)


def _band_bias_tiles(rel_bias):
    xb = jnp.arange(BAND_KEY_BLOCKS)[:, None, None]
    r = jnp.arange(ATT_BLOCK)[None, :, None]
    c = jnp.arange(ATT_BLOCK)[None, None, :]
    back = (BAND_KEY_BLOCKS - 1) - xb
    rel = back * ATT_BLOCK + r - c
    chunk_back = back * (ATT_BLOCK // CHUNK) + r // CHUNK - c // CHUNK
    visible = (chunk_back >= 0) & (chunk_back < BAND_CHUNKS)
    vals = rel_bias.astype(F32)[:, jnp.clip(rel, -REL_CLIP, REL_CLIP) + REL_CLIP]
    return jnp.where(visible[None], vals, NEG_BIG)


def _mla_prep_kernel(cdq_ref, ckv_ref, qg_ref, kvg_ref, w1_ref, w2_ref, wk_ref, wv_ref, ct_ref, st_ref,
                     q_out, k_out, v_out):
    ct = ct_ref[...]
    st = st_ref[...]
    cq = cdq_ref[...].astype(F32)
    nq = cq * lax.rsqrt(jnp.mean(cq * cq, axis=-1, keepdims=True) + 1e-6) * qg_ref[...]
    nq = nq.astype(BF16)
    q1 = _dot(nq, w1_ref[...])
    q2 = _dot(nq, w2_ref[...])
    ckv = ckv_ref[...].astype(F32)
    lat = ckv[:, :MLA_KV_RANK]
    nkv = lat * lax.rsqrt(jnp.mean(lat * lat, axis=-1, keepdims=True) + 1e-6) * kvg_ref[...]
    nkv = nkv.astype(BF16)
    kn = _dot(nkv, wk_ref[...])
    v_out[...] = _dot(nkv, wv_ref[...]).astype(v_out.dtype)
    kr = ckv[:, MLA_KV_RANK:]
    kr = (kr * ct + pltpu.roll(kr, MLA_ROPE, axis=1) * st).astype(k_out.dtype)
    for h in range(N_HEADS):
        a, bq = 2 * LANE * h, LANE * h
        q_out[:, a:a + LANE] = q1[:, a:a + LANE].astype(q_out.dtype)
        q_out[:, a + LANE:a + 2 * LANE] = (q1[:, a + LANE:a + 2 * LANE] * ct
                                           + q2[:, bq:bq + LANE] * st).astype(q_out.dtype)
        k_out[:, a:a + LANE] = kn[:, bq:bq + LANE].astype(k_out.dtype)
        k_out[:, a + LANE:a + 2 * LANE] = kr


def _mla_prep(h2d, q_gain, kv_gain, w1, w2, wk, wv, cos_t, sin_t, b, t):
    m = b * t
    tm = 256
    nt = t // tm
    full = lambda shape: pl.BlockSpec(shape, lambda i: (0, 0))
    return pl.pallas_call(
        _mla_prep_kernel,
        grid=(m // tm,),
        in_specs=[pl.BlockSpec((tm, MLA_Q_RANK), lambda i: (i, COL_MLA_Q * LANE // MLA_Q_RANK)),
                  pl.BlockSpec((tm, 3 * LANE), lambda i: (i, COL_MLA_KV // 3)),
                  full((1, MLA_Q_RANK)), full((1, MLA_KV_RANK)),
                  full(w1.shape), full(w2.shape), full(wk.shape), full(wv.shape),
                  pl.BlockSpec((tm, LANE), lambda i: (i % nt, 0)),
                  pl.BlockSpec((tm, LANE), lambda i: (i % nt, 0))],
        out_specs=[pl.BlockSpec((tm, 2 * GROUP_WIDTH), lambda i: (i, 0)),
                   pl.BlockSpec((tm, 2 * GROUP_WIDTH), lambda i: (i, 0)),
                   pl.BlockSpec((tm, GROUP_WIDTH), lambda i: (i, 0))],
        out_shape=[jax.ShapeDtypeStruct((m, 2 * GROUP_WIDTH), BF16),
                   jax.ShapeDtypeStruct((m, 2 * GROUP_WIDTH), BF16),
                   jax.ShapeDtypeStruct((m, GROUP_WIDTH), BF16)],
        compiler_params=_params("arbitrary"),
        name="mla_prep",
    )(h2d, h2d, q_gain, kv_gain, w1, w2, wk, wv, cos_t, sin_t)


HALO = 16


def _gdn_prep_kernel(blocks_per_seq, x_ref, halo_ref, ab_ref, cw_ref, alog_ref, dtb_ref,
                     qn_ref, kn_ref, v_ref, gcol_ref, bcol_ref, grow_ref):
    i = pl.program_id(0)
    halo_scale = jnp.where(i % blocks_per_seq == 0, 0.0, 1.0)
    outs = (qn_ref, kn_ref, v_ref)
    for j in range(3 * N_HEADS):
        sl = slice(LANE * j, LANE * (j + 1))
        cat = jnp.concatenate([halo_ref[:, sl].astype(F32) * halo_scale, x_ref[:, sl].astype(F32)], axis=0)
        w = cw_ref[:, sl]
        acc = cat * w[GDN_CONV - 1:GDN_CONV]
        for s in range(1, GDN_CONV):
            acc = acc + pltpu.roll(cat, s, axis=0) * w[GDN_CONV - 1 - s:GDN_CONV - s]
        y = _silu(acc[HALO:])
        part, head = divmod(j, N_HEADS)
        if part < 2:
            y = y * lax.rsqrt(jnp.sum(y * y, axis=-1, keepdims=True) + 1e-6)
        if part == 0:
            y = y * HEAD_DIM ** -0.5
        outs[part][:, LANE * head:LANE * (head + 1)] = y.astype(BF16)

    ab = ab_ref[...].astype(F32)
    z = ab + dtb_ref[...]
    softplus = jnp.maximum(z, 0.0) + jnp.log1p(jnp.exp(-jnp.abs(z)))
    g = -jnp.exp(alog_ref[...]) * softplus
    beta = 1.0 / (1.0 + jnp.exp(-ab))
    row = lax.broadcasted_iota(jnp.int32, g.shape, 0) % CHUNK
    s = 1
    while s < CHUNK:
        g = g + jnp.where(row >= s, pltpu.roll(g, s, axis=0), 0.0)
        s *= 2
    gt = g.T
    for h in range(N_HEADS):
        gcol_ref[h] = jnp.broadcast_to(g[:, h:h + 1], g.shape)
        bcol_ref[h] = jnp.broadcast_to(beta[:, N_HEADS + h:N_HEADS + h + 1], g.shape)
        grow_ref[h] = gt[h:h + 1, :]


def _gdn_prep(h2d, conv_w, alog, dtb, b, t):
    m = b * t
    tm = GDN_GROUP
    nt = t // tm
    return pl.pallas_call(
        functools.partial(_gdn_prep_kernel, nt),
        grid=(m // tm,),
        in_specs=[pl.BlockSpec((tm, 3 * GROUP_WIDTH), lambda i: (i, 0)),
                  pl.BlockSpec((HALO, 3 * GROUP_WIDTH), lambda i: (jnp.maximum(i * (tm // HALO) - 1, 0), 0)),
                  pl.BlockSpec((tm, LANE), lambda i: (i, COL_GDN_AB)),
                  pl.BlockSpec((GDN_CONV, 3 * GROUP_WIDTH), lambda i: (0, 0)),
                  pl.BlockSpec((1, LANE), lambda i: (0, 0)),
                  pl.BlockSpec((1, LANE), lambda i: (0, 0))],
        out_specs=[pl.BlockSpec((tm, GROUP_WIDTH), lambda i: (i, 0)),
                   pl.BlockSpec((tm, GROUP_WIDTH), lambda i: (i, 0)),
                   pl.BlockSpec((tm, GROUP_WIDTH), lambda i: (i, 0)),
                   pl.BlockSpec((None, N_HEADS, tm, LANE), lambda i: (i // nt, 0, i % nt, 0)),
                   pl.BlockSpec((None, N_HEADS, tm, LANE), lambda i: (i // nt, 0, i % nt, 0)),
                   pl.BlockSpec((None, N_HEADS, 1, tm), lambda i: (i // nt, 0, 0, i % nt))],
        out_shape=[jax.ShapeDtypeStruct((m, GROUP_WIDTH), BF16),
                   jax.ShapeDtypeStruct((m, GROUP_WIDTH), BF16),
                   jax.ShapeDtypeStruct((m, GROUP_WIDTH), BF16),
                   jax.ShapeDtypeStruct((b, N_HEADS, t, LANE), F32),
                   jax.ShapeDtypeStruct((b, N_HEADS, t, LANE), F32),
                   jax.ShapeDtypeStruct((b, N_HEADS, 1, t), F32)],
        compiler_params=_params("arbitrary"),
        name="gdn_prep",
    )(h2d, h2d, h2d, conv_w, alog, dtb)


def _gdn_kernel(q_ref, k_ref, v_ref, gc_ref, bc_ref, gr_ref, gate_ref, nw_ref, o_ref, state_ref, vnew_ref):
    n = pl.program_id(2)

    @pl.when(n == 0)
    def _():
        state_ref[...] = jnp.zeros_like(state_ref)

    g_sz = GDN_GROUP
    n_chunks = g_sz // CHUNK
    q16 = q_ref[...]
    k16 = k_ref[...]
    kf = k16.astype(F32)
    gc = gc_ref[...]
    beta = bc_ref[...]
    eg = jnp.exp(gc)
    kbeta = kf * beta

    ri = lax.broadcasted_iota(jnp.int32, (g_sz, g_sz), 0)
    ci = lax.broadcasted_iota(jnp.int32, (g_sz, g_sz), 1)
    dif = jnp.where((ri // CHUNK) == (ci // CHUNK), ri - ci, -1)
    incl = dif >= 0
    e = jnp.concatenate([gc, gc], axis=1) - gr_ref[...]
    decay = jnp.where(incl, jnp.exp(jnp.where(incl, e, 0.0)), 0.0)

    mpow = jnp.where(dif > 0, -(_dot_nt(kbeta.astype(BF16), k16) * decay), 0.0)
    inv = jnp.where(dif == 0, 1.0, 0.0) + mpow
    mpow16 = mpow.astype(BF16)
    for _ in range(int(math.log2(CHUNK)) - 1):
        mpow16 = _dot(mpow16, mpow16).astype(BF16)
        inv = inv + _dot(inv.astype(BF16), mpow16)

    rhs = jnp.concatenate([v_ref[...].astype(F32) * beta, kbeta * eg], axis=1).astype(BF16)
    uw = _dot(inv.astype(BF16), rhs)
    u = uw[:, :HEAD_DIM]
    w = uw[:, HEAD_DIM:]
    qk = (_dot_nt(q16, k16) * decay).astype(BF16)
    qe = (q16.astype(F32) * eg).astype(BF16)
    g_last = jnp.concatenate(
        [jnp.broadcast_to(gc[CHUNK * (c + 1) - 1:CHUNK * (c + 1), :], (CHUNK, LANE)) for c in range(n_chunks)], axis=0)
    kdec_t = (kf * jnp.exp(g_last - gc)).T.astype(BF16)
    col_chunk = lax.broadcasted_iota(jnp.int32, kdec_t.shape, 1) // CHUNK

    vnew_ref[...] = u
    state = state_ref[...]
    outs = []
    for c in range(n_chunks):
        rows = slice(CHUNK * c, CHUNK * (c + 1))
        s16 = state.astype(BF16)
        vnew_ref[rows, :] = u[rows] - _dot(w[rows].astype(BF16), s16)
        vfull = vnew_ref[...].astype(BF16)
        outs.append(_dot(qe[rows], s16) + _dot(qk[rows], vfull))
        state = (state * jnp.exp(gc[CHUNK * (c + 1) - 1:CHUNK * (c + 1), :])
                 + _dot(jnp.where(col_chunk == c, kdec_t, jnp.zeros_like(kdec_t)), vfull))
    state_ref[...] = state
    o = jnp.concatenate(outs, axis=0)
    o = o * lax.rsqrt(jnp.mean(o * o, axis=-1, keepdims=True) + 1e-6) * nw_ref[...]
    o_ref[...] = (o * _silu(gate_ref[...].astype(F32))).astype(o_ref.dtype)


def _gated_delta_net(qn, kn, vv, gcol, bcol, grow, h2d, norm_w, b, t):
    m = b * t
    ng = t // GDN_GROUP
    blk = (GDN_GROUP, HEAD_DIM)
    tok = lambda bi, h, n: (bi * ng + n, h)
    return pl.pallas_call(
        _gdn_kernel,
        grid=(b, N_HEADS, ng),
        in_specs=[pl.BlockSpec(blk, tok), pl.BlockSpec(blk, tok), pl.BlockSpec(blk, tok),
                  pl.BlockSpec((None, None, GDN_GROUP, LANE), lambda bi, h, n: (bi, h, n, 0)),
                  pl.BlockSpec((None, None, GDN_GROUP, LANE), lambda bi, h, n: (bi, h, n, 0)),
                  pl.BlockSpec((None, None, 1, GDN_GROUP), lambda bi, h, n: (bi, h, 0, n)),
                  pl.BlockSpec(blk, lambda bi, h, n: (bi * ng + n, COL_GATE + 8 + h)),
                  pl.BlockSpec((1, HEAD_DIM), lambda bi, h, n: (0, 0))],
        out_specs=pl.BlockSpec(blk, tok),
        out_shape=jax.ShapeDtypeStruct((m, GROUP_WIDTH), BF16),
        scratch_shapes=[pltpu.VMEM((HEAD_DIM, HEAD_DIM), F32), pltpu.VMEM((GDN_GROUP, HEAD_DIM), F32)],
        compiler_params=_params("arbitrary", "arbitrary", "arbitrary"),
        name="gated_delta_net",
    )(qn, kn, vv, gcol, bcol, grow, h2d, norm_w)


def _out_proj_kernel(n_col_tiles, oa_ref, ob_ref, oc_ref, od_ref, w_ref, x_ref, g_ref, b_ref,
                     y_ref, y16_ref, acc_ref):
    j = pl.program_id(1)
    o = jnp.concatenate([oa_ref[...], ob_ref[...], oc_ref[...], od_ref[...]], axis=1)
    acc_ref[j] = _dot(o, w_ref[...]) + DEEPNORM_ALPHA * x_ref[...]

    @pl.when(j == n_col_tiles - 1)
    def _():
        tn = acc_ref.shape[2]
        total = acc_ref[0].sum(axis=-1, keepdims=True)
        for jj in range(1, n_col_tiles):
            total = total + acc_ref[jj].sum(axis=-1, keepdims=True)
        mean = total * (1.0 / D_MODEL)
        sq = jnp.zeros_like(mean)
        for jj in range(n_col_tiles):
            xc = acc_ref[jj] - mean
            sq = sq + (xc * xc).sum(axis=-1, keepdims=True)
        inv = lax.rsqrt(sq * (1.0 / D_MODEL) + 1e-5)
        for jj in range(n_col_tiles):
            sl = slice(tn * jj, tn * (jj + 1))
            y = (acc_ref[jj] - mean) * inv * g_ref[:, sl] + b_ref[:, sl]
            y_ref[:, sl] = y
            y16_ref[:, sl] = y.astype(BF16)


def _out_proj_ln(oa, ob, oc, od, w16, x2d, gain, bias):
    m = x2d.shape[0]
    tm, tn = 256, 1024
    nj = D_MODEL // tn
    grp = pl.BlockSpec((tm, GROUP_WIDTH), lambda i, j: (i, 0))
    row = pl.BlockSpec((tm, D_MODEL), lambda i, j: (i, 0))
    vec = pl.BlockSpec((1, D_MODEL), lambda i, j: (0, 0))
    return pl.pallas_call(
        functools.partial(_out_proj_kernel, nj),
        grid=(m // tm, nj),
        in_specs=[grp, grp, grp, grp,
                  pl.BlockSpec((D_MODEL, tn), lambda i, j: (0, j)),
                  pl.BlockSpec((tm, tn), lambda i, j: (i, j)),
                  vec, vec],
        out_specs=[row, row],
        out_shape=[jax.ShapeDtypeStruct((m, D_MODEL), F32), jax.ShapeDtypeStruct((m, D_MODEL), BF16)],
        scratch_shapes=[pltpu.VMEM((nj, tm, tn), F32)],
        compiler_params=_params("arbitrary", "arbitrary"),
        name="out_proj_layernorm",
    )(oa, ob, oc, od, w16, x2d, gain, bias)


def _rotate_half_cols(w):
    half = MLA_ROPE // 2
    return jnp.concatenate([-w[:, half:], w[:, :half]], axis=1)


def _layout_w_in(w):
    (a_q, a_k, a_v, b_qkv, b_a, b_b, c_dq, c_dkv, d_q, d_k, d_v, gate) = jnp.split(w, _split_points(), axis=1)
    ab = jnp.concatenate([b_a, b_b, jnp.zeros((w.shape[0], LANE - 2 * N_HEADS), w.dtype)], axis=1)
    k_rot = _rotate_half_cols(c_dkv[:, MLA_KV_RANK:])
    out = jnp.concatenate([b_qkv, c_dq, c_dkv, k_rot, ab, a_q, a_k, a_v, d_q, d_k, d_v, gate], axis=1)
    return out.astype(BF16)


def _split_points():
    pts, s = [], 0
    for n in IN_SIZES[:-1]:
        s += n
        pts.append(s)
    return pts


def _layout_w_uq(w):
    w = w.reshape(MLA_Q_RANK, N_HEADS, MLA_NOPE + MLA_ROPE)
    zeros = jnp.zeros((MLA_Q_RANK, N_HEADS, LANE - MLA_ROPE), w.dtype)
    rope = w[:, :, MLA_NOPE:]
    rot = jnp.concatenate([-rope[:, :, MLA_ROPE // 2:], rope[:, :, :MLA_ROPE // 2]], axis=2)
    w1 = jnp.concatenate([w, zeros], axis=2).reshape(MLA_Q_RANK, N_HEADS * 2 * LANE)
    w2 = jnp.concatenate([rot, zeros], axis=2).reshape(MLA_Q_RANK, N_HEADS * LANE)
    return w1.astype(BF16), w2.astype(BF16)


def _layout_w_ukv(w):
    w = w.reshape(MLA_KV_RANK, N_HEADS, MLA_NOPE + HEAD_DIM)
    wk = w[:, :, :MLA_NOPE].reshape(MLA_KV_RANK, GROUP_WIDTH)
    wv = w[:, :, MLA_NOPE:].reshape(MLA_KV_RANK, GROUP_WIDTH)
    return wk.astype(BF16), wv.astype(BF16)


def _rope_tables(t):
    half = MLA_ROPE // 2
    inv = ROPE_THETA ** (-jnp.arange(half, dtype=F32) / half)
    ang = jnp.arange(t).astype(F32)[:, None] * inv[None, :]
    zeros = jnp.zeros((t, LANE - MLA_ROPE), F32)
    cos_t = jnp.concatenate([jnp.cos(ang), jnp.cos(ang), zeros], axis=1)
    sin_t = jnp.concatenate([jnp.sin(ang), jnp.sin(ang), zeros], axis=1)
    return cos_t, sin_t


def _pad_lanes(v):
    return jnp.concatenate([v.astype(F32), jnp.zeros((LANE - v.shape[0],), F32)])[None, :]


def _layer(x2d, x16, layer_idx, b, t, w_in, diff_lambda, diff_norm, gdn_conv, gdn_a_log, gdn_dt_bias, gdn_norm,
           mla_q_norm, mla_w_uq, mla_kv_norm, mla_w_ukv, rel_bias, w_out, ln_gain, ln_bias, tables):
    h2d = _in_proj(x16, _layout_w_in(w_in))

    lam_init = 0.8 - 0.6 * math.exp(-0.3 * layer_idx)
    slopes = 2.0 ** (-8.0 * jnp.arange(1, N_HEADS + 1, dtype=F32) / N_HEADS)
    slopes = jnp.broadcast_to(slopes[:, None, None], (N_HEADS, 1, LANE))
    lam_p = jnp.concatenate([diff_lambda.astype(F32), jnp.zeros((4, LANE - DIFF_HALF), F32)], axis=1)
    o_a = _diff_attention(h2d, slopes, lam_p, diff_norm.astype(F32)[None, :], lam_init, b, t)

    qn, kn, vv, gcol, bcol, grow = _gdn_prep(h2d, gdn_conv.astype(F32), _pad_lanes(gdn_a_log),
                                              _pad_lanes(gdn_dt_bias), b, t)
    o_b = _gated_delta_net(qn, kn, vv, gcol, bcol, grow, h2d, gdn_norm.astype(F32)[None, :], b, t)

    w1, w2 = _layout_w_uq(mla_w_uq)
    wk, wv = _layout_w_ukv(mla_w_ukv)
    cos_t, sin_t = tables
    qf, kf, vc = _mla_prep(h2d, mla_q_norm.astype(F32)[None, :], mla_kv_norm.astype(F32)[None, :],
                           w1, w2, wk, wv, cos_t, sin_t, b, t)
    o_c = _mla_attention(qf, kf, vc, h2d, b, t)

    o_d = _band_attention(h2d, _band_bias_tiles(rel_bias), b, t)

    return _out_proj_ln(o_a, o_b, o_c, o_d, w_out.astype(BF16), x2d,
                        ln_gain.astype(F32)[None, :], ln_bias.astype(F32)[None, :])


def kernel(x, w_in, diff_lambda, diff_norm, gdn_conv, gdn_a_log, gdn_dt_bias, gdn_norm, mla_q_norm, mla_w_uq,
           mla_kv_norm, mla_w_ukv, rel_bias, w_out, ln_gain, ln_bias):
    b, t, d = x.shape
    assert d == D_MODEL and t % GDN_GROUP == 0 and (b * t) % 512 == 0
    tables = _rope_tables(t)
    x2d = x.reshape(b * t, d)
    x16 = x2d.astype(BF16)
    for l in range(DEPTH):
        x2d, x16 = _layer(x2d, x16, l, b, t, w_in[l], diff_lambda[l], diff_norm[l], gdn_conv[l], gdn_a_log[l],
                          gdn_dt_bias[l], gdn_norm[l], mla_q_norm[l], mla_w_uq[l], mla_kv_norm[l],
                          mla_w_ukv[l], rel_bias[l], w_out[l], ln_gain[l], ln_bias[l], tables)
    return x2d.reshape(b, t, d)
```

```python
import functools
import math

import jax
import jax.numpy as jnp
from jax import lax
from jax.experimental import pallas as pl
from jax.experimental.pallas import tpu as pltpu

F32 = jnp.float32
BF16 = jnp.bfloat16

D_MODEL = 4096
DEPTH = 2
CHUNK = 64
N_HEADS = 8
HEAD_DIM = 128
GROUP_WIDTH = N_HEADS * HEAD_DIM
DIFF_HALF = HEAD_DIM // 2
GDN_CONV = 4
MLA_Q_RANK = 768
MLA_KV_RANK = 256
MLA_NOPE = 128
MLA_ROPE = 64
ROPE_THETA = 10000.0
BAND_CHUNKS = 9
REL_CLIP = 128
DEEPNORM_ALPHA = (2 * DEPTH) ** 0.25
IN_SIZES = (GROUP_WIDTH, GROUP_WIDTH, GROUP_WIDTH, 3 * GROUP_WIDTH, N_HEADS, N_HEADS,
            MLA_Q_RANK, MLA_KV_RANK + MLA_ROPE, GROUP_WIDTH, GROUP_WIDTH, GROUP_WIDTH, D_MODEL)

LANE = 128
COL_GDN_QKV = 0
COL_MLA_Q = 24
COL_MLA_KV = 30
COL_GDN_AB = 33
COL_DIFF_Q = 34
COL_DIFF_K = 42
COL_DIFF_V = 50
COL_BAND_Q = 58
COL_BAND_K = 66
COL_BAND_V = 74
COL_GATE = 82
N_COL_BLOCKS = 114
N_COLS = N_COL_BLOCKS * LANE

ATT_BLOCK = 256
GDN_GROUP = 256
NEG_BIG = -1e30
VMEM_LIMIT = 48 * 1024 * 1024


def _params(*sem):
    return pltpu.CompilerParams(dimension_semantics=sem, vmem_limit_bytes=VMEM_LIMIT)


def _dot(a, b):
    return jnp.dot(a, b, preferred_element_type=F32)


def _dot_nt(a, b):
    return lax.dot_general(a, b, (((1,), (1,)), ((), ())), preferred_element_type=F32)


def _silu(x):
    return x * (1.0 / (1.0 + jnp.exp(-x)))


def _matmul_kernel(x_ref, w_ref, o_ref):
    o_ref[...] = _dot(x_ref[...], w_ref[...]).astype(o_ref.dtype)


def _in_proj(xb, wb):
    m = xb.shape[0]
    tm, tn = 512, 768
    return pl.pallas_call(
        _matmul_kernel,
        grid=(N_COLS // tn, m // tm),
        in_specs=[pl.BlockSpec((tm, D_MODEL), lambda j, i: (i, 0)),
                  pl.BlockSpec((D_MODEL, tn), lambda j, i: (0, j))],
        out_specs=pl.BlockSpec((tm, tn), lambda j, i: (i, j)),
        out_shape=jax.ShapeDtypeStruct((m, N_COLS), BF16),
        compiler_params=_params("arbitrary", "arbitrary"),
        name="in_proj",
    )(xb, wb)


def _softmax_stats(m, l, t, shift):
    m_new = jnp.maximum(m, jnp.max(t, axis=-1, keepdims=True) + shift)
    alpha = jnp.exp(m - m_new)
    p = jnp.exp(t - (m_new - shift))
    return m_new, alpha * l + jnp.sum(p, axis=-1, keepdims=True), alpha, p.astype(BF16)


def _attn_sweep(qs, k_ref, v_ref, lo, i, off_scores, diag_scores):
    tq = qs[0].shape[0]

    def rows(ref, kb):
        return ref[pl.ds(pl.multiple_of(kb * ATT_BLOCK, ATT_BLOCK), ATT_BLOCK), :]

    def body(kb, carry):
        k_next = rows(k_ref, kb + 1)
        v_prev = rows(v_ref, jnp.maximum(kb - 1, lo))
        new = []
        for (m, l, acc, s, p_prev, a_prev), q in zip(carry, qs):
            s_next = _dot_nt(q, k_next)
            acc = a_prev * acc + _dot(p_prev, v_prev)
            m, l, alpha, p = _softmax_stats(m, l, *off_scores(s, kb))
            new.append((m, l, acc, s_next, p, alpha))
        return tuple(new)

    k_lo = rows(k_ref, lo)
    init = tuple((jnp.full((tq, 1), NEG_BIG, F32), jnp.zeros((tq, 1), F32), jnp.zeros((tq, HEAD_DIM), F32),
                  _dot_nt(q, k_lo), jnp.zeros((tq, ATT_BLOCK), BF16), jnp.ones((tq, 1), F32)) for q in qs)
    carry = lax.fori_loop(lo, i, body, init)
    v_prev = rows(v_ref, jnp.maximum(i - 1, lo))
    v_diag = rows(v_ref, i)
    out = []
    for m, l, acc, s, p_prev, a_prev in carry:
        acc = a_prev * acc + _dot(p_prev, v_prev)
        m, l, alpha, p = _softmax_stats(m, l, diag_scores(s), 0.0)
        out.append((alpha * acc + _dot(p, v_diag), l))
    return out


def _block_iotas():
    r = lax.broadcasted_iota(jnp.int32, (ATT_BLOCK, ATT_BLOCK), 0)
    c = lax.broadcasted_iota(jnp.int32, (ATT_BLOCK, ATT_BLOCK), 1)
    return r, c


def _diff_attn_kernel(lam_init, q_ref, k_ref, v_ref, gate_ref, slope_ref, lam_ref, nw_ref, o_ref):
    i = pl.program_id(2)
    q = q_ref[...] * (DIFF_HALF ** -0.5)
    lane = lax.broadcasted_iota(jnp.int32, q.shape, 1)
    q1 = jnp.where(lane < DIFF_HALF, q, jnp.zeros_like(q))
    q2 = jnp.where(lane >= DIFF_HALF, q, jnp.zeros_like(q))
    r, c = _block_iotas()
    rc = (r - c).astype(F32)
    slope = slope_ref[:, 0:1]
    bias_off = -slope * rc
    bias_diag = jnp.where((c // CHUNK) <= (r // CHUNK), -slope * jnp.abs(rc), NEG_BIG)

    def off_scores(s, kb):
        return s + bias_off, -slope * ((i - kb) * ATT_BLOCK).astype(F32)

    def diag_scores(s):
        return s + bias_diag

    (acc1, l1), (acc2, l2) = _attn_sweep((q1, q2), k_ref, v_ref, 0, i, off_scores, diag_scores)
    lp = lam_ref[...]
    lam = (jnp.exp(jnp.sum(lp[0:1] * lp[1:2], axis=-1, keepdims=True))
           - jnp.exp(jnp.sum(lp[2:3] * lp[3:4], axis=-1, keepdims=True)) + lam_init)
    o = acc1 / l1 - lam * (acc2 / l2)
    o = o * lax.rsqrt(jnp.mean(o * o, axis=-1, keepdims=True) + 1e-6) * nw_ref[...]
    o = o * (1.0 - lam_init)
    o_ref[...] = (o * _silu(gate_ref[...].astype(F32))).astype(o_ref.dtype)


def _diff_attention(h2d, slopes, lam_p, norm_w, lam_init, b, t):
    nq = t // ATT_BLOCK
    m = b * t
    blk = (ATT_BLOCK, HEAD_DIM)
    return pl.pallas_call(
        functools.partial(_diff_attn_kernel, lam_init),
        grid=(b, N_HEADS, nq),
        in_specs=[pl.BlockSpec(blk, lambda bi, h, i: (bi * nq + i, COL_DIFF_Q + h)),
                  pl.BlockSpec((t, HEAD_DIM), lambda bi, h, i: (bi, COL_DIFF_K + h)),
                  pl.BlockSpec((t, HEAD_DIM), lambda bi, h, i: (bi, COL_DIFF_V + h)),
                  pl.BlockSpec(blk, lambda bi, h, i: (bi * nq + i, COL_GATE + h)),
                  pl.BlockSpec((None, 1, LANE), lambda bi, h, i: (h, 0, 0)),
                  pl.BlockSpec((4, LANE), lambda bi, h, i: (0, 0)),
                  pl.BlockSpec((1, HEAD_DIM), lambda bi, h, i: (0, 0))],
        out_specs=pl.BlockSpec(blk, lambda bi, h, i: (bi * nq + i, h)),
        out_shape=jax.ShapeDtypeStruct((m, GROUP_WIDTH), BF16),
        compiler_params=_params("arbitrary", "arbitrary", "arbitrary"),
        name="diff_attention",
    )(h2d, h2d, h2d, h2d, slopes, lam_p, norm_w)


def _mla_attn_kernel(q_ref, k_ref, v_ref, gate_ref, o_ref):
    i = pl.program_id(2)
    r, c = _block_iotas()
    mask_diag = jnp.where((c // CHUNK) <= (r // CHUNK), 0.0, NEG_BIG)
    scale = (MLA_NOPE + MLA_ROPE) ** -0.5

    def off_scores(s, kb):
        return s * scale, 0.0

    def diag_scores(s):
        return s * scale + mask_diag

    ((acc, l),) = _attn_sweep((q_ref[...],), k_ref, v_ref, 0, i, off_scores, diag_scores)
    o_ref[...] = ((acc / l) * _silu(gate_ref[...].astype(F32))).astype(o_ref.dtype)


def _mla_attention(qf, kf, vv, h2d, b, t):
    nq = t // ATT_BLOCK
    m = b * t
    return pl.pallas_call(
        _mla_attn_kernel,
        grid=(b, N_HEADS, nq),
        in_specs=[pl.BlockSpec((ATT_BLOCK, 2 * LANE), lambda bi, h, i: (bi * nq + i, h)),
                  pl.BlockSpec((t, 2 * LANE), lambda bi, h, i: (bi, h)),
                  pl.BlockSpec((t, HEAD_DIM), lambda bi, h, i: (bi, h)),
                  pl.BlockSpec((ATT_BLOCK, HEAD_DIM), lambda bi, h, i: (bi * nq + i, COL_GATE + 16 + h))],
        out_specs=pl.BlockSpec((ATT_BLOCK, HEAD_DIM), lambda bi, h, i: (bi * nq + i, h)),
        out_shape=jax.ShapeDtypeStruct((m, GROUP_WIDTH), BF16),
        compiler_params=_params("arbitrary", "arbitrary", "arbitrary"),
        name="mla_attention",
    )(qf, kf, vv, h2d)


BAND_KEY_BLOCKS = (BAND_CHUNKS - 1) * CHUNK // ATT_BLOCK + 1


def _band_attn_kernel(q_ref, k_ref, v_ref, gate_ref, bias_ref, o_ref):
    i = pl.program_id(2)
    scale = HEAD_DIM ** -0.5

    def off_scores(s, kb):
        return s * scale + bias_ref[kb - i + (BAND_KEY_BLOCKS - 1)], 0.0

    def diag_scores(s):
        return s * scale + bias_ref[BAND_KEY_BLOCKS - 1]

    lo = jnp.maximum(i - (BAND_KEY_BLOCKS - 1), 0)
    ((acc, l),) = _attn_sweep((q_ref[...],), k_ref, v_ref, lo, i, off_scores, diag_scores)
    o_ref[...] = ((acc / l) * _silu(gate_ref[...].astype(F32))).astype(o_ref.dtype)


def _band_attention(h2d, bias_tiles, b, t):
    nq = t // ATT_BLOCK
    m = b * t
    blk = (ATT_BLOCK, HEAD_DIM)
    return pl.pallas_call(
        _band_attn_kernel,
        grid=(b, N_HEADS, nq),
        in_specs=[pl.BlockSpec(blk, lambda bi, h, i: (bi * nq + i, COL_BAND_Q + h)),
                  pl.BlockSpec((t, HEAD_DIM), lambda bi, h, i: (bi, COL_BAND_K + h)),
                  pl.BlockSpec((t, HEAD_DIM), lambda bi, h, i: (bi, COL_BAND_V + h)),
                  pl.BlockSpec(blk, lambda bi, h, i: (bi * nq + i, COL_GATE + 24 + h)),
                  pl.BlockSpec((None, BAND_KEY_BLOCKS, ATT_BLOCK, ATT_BLOCK), lambda bi, h, i: (h, 0, 0, 0))],
        out_specs=pl.BlockSpec(blk, lambda bi, h, i: (bi * nq + i, h)),
        out_shape=jax.ShapeDtypeStruct((m, GROUP_WIDTH), BF16),
        compiler_params=_params("arbitrary", "arbitrary", "arbitrary"),
        name="band_attention",
    )(h2d, h2d, h2d, h2d, bias_tiles)


def _band_bias_tiles(rel_bias):
    xb = jnp.arange(BAND_KEY_BLOCKS)[:, None, None]
    r = jnp.arange(ATT_BLOCK)[None, :, None]
    c = jnp.arange(ATT_BLOCK)[None, None, :]
    back = (BAND_KEY_BLOCKS - 1) - xb
    rel = back * ATT_BLOCK + r - c
    chunk_back = back * (ATT_BLOCK // CHUNK) + r // CHUNK - c // CHUNK
    visible = (chunk_back >= 0) & (chunk_back < BAND_CHUNKS)
    vals = rel_bias.astype(F32)[:, jnp.clip(rel, -REL_CLIP, REL_CLIP) + REL_CLIP]
    return jnp.where(visible[None], vals, NEG_BIG)


def _mla_prep_kernel(cdq_ref, ckv_ref, qg_ref, kvg_ref, w1_ref, w2_ref, wk_ref, wv_ref, ct_ref, st_ref,
                     q_out, k_out, v_out):
    ct = ct_ref[...]
    st = st_ref[...]
    cq = cdq_ref[...].astype(F32)
    nq = cq * lax.rsqrt(jnp.mean(cq * cq, axis=-1, keepdims=True) + 1e-6) * qg_ref[...]
    nq = nq.astype(BF16)
    q1 = _dot(nq, w1_ref[...])
    q2 = _dot(nq, w2_ref[...])
    ckv = ckv_ref[...].astype(F32)
    lat = ckv[:, :MLA_KV_RANK]
    nkv = lat * lax.rsqrt(jnp.mean(lat * lat, axis=-1, keepdims=True) + 1e-6) * kvg_ref[...]
    nkv = nkv.astype(BF16)
    kn = _dot(nkv, wk_ref[...])
    v_out[...] = _dot(nkv, wv_ref[...]).astype(v_out.dtype)
    kr = ckv[:, MLA_KV_RANK:]
    kr = (kr * ct + pltpu.roll(kr, MLA_ROPE, axis=1) * st).astype(k_out.dtype)
    for h in range(N_HEADS):
        a, bq = 2 * LANE * h, LANE * h
        q_out[:, a:a + LANE] = q1[:, a:a + LANE].astype(q_out.dtype)
        q_out[:, a + LANE:a + 2 * LANE] = (q1[:, a + LANE:a + 2 * LANE] * ct
                                           + q2[:, bq:bq + LANE] * st).astype(q_out.dtype)
        k_out[:, a:a + LANE] = kn[:, bq:bq + LANE].astype(k_out.dtype)
        k_out[:, a + LANE:a + 2 * LANE] = kr


def _mla_prep(h2d, q_gain, kv_gain, w1, w2, wk, wv, cos_t, sin_t, b, t):
    m = b * t
    tm = 256
    nt = t // tm
    full = lambda shape: pl.BlockSpec(shape, lambda i: (0, 0))
    return pl.pallas_call(
        _mla_prep_kernel,
        grid=(m // tm,),
        in_specs=[pl.BlockSpec((tm, MLA_Q_RANK), lambda i: (i, COL_MLA_Q * LANE // MLA_Q_RANK)),
                  pl.BlockSpec((tm, 3 * LANE), lambda i: (i, COL_MLA_KV // 3)),
                  full((1, MLA_Q_RANK)), full((1, MLA_KV_RANK)),
                  full(w1.shape), full(w2.shape), full(wk.shape), full(wv.shape),
                  pl.BlockSpec((tm, LANE), lambda i: (i % nt, 0)),
                  pl.BlockSpec((tm, LANE), lambda i: (i % nt, 0))],
        out_specs=[pl.BlockSpec((tm, 2 * GROUP_WIDTH), lambda i: (i, 0)),
                   pl.BlockSpec((tm, 2 * GROUP_WIDTH), lambda i: (i, 0)),
                   pl.BlockSpec((tm, GROUP_WIDTH), lambda i: (i, 0))],
        out_shape=[jax.ShapeDtypeStruct((m, 2 * GROUP_WIDTH), BF16),
                   jax.ShapeDtypeStruct((m, 2 * GROUP_WIDTH), BF16),
                   jax.ShapeDtypeStruct((m, GROUP_WIDTH), BF16)],
        compiler_params=_params("arbitrary"),
        name="mla_prep",
    )(h2d, h2d, q_gain, kv_gain, w1, w2, wk, wv, cos_t, sin_t)


HALO = 16


def _gdn_prep_kernel(blocks_per_seq, x_ref, halo_ref, ab_ref, cw_ref, alog_ref, dtb_ref,
                     qn_ref, kn_ref, v_ref, gcol_ref, bcol_ref, grow_ref):
    i = pl.program_id(0)
    halo_scale = jnp.where(i % blocks_per_seq == 0, 0.0, 1.0)
    outs = (qn_ref, kn_ref, v_ref)
    for j in range(3 * N_HEADS):
        sl = slice(LANE * j, LANE * (j + 1))
        cat = jnp.concatenate([halo_ref[:, sl].astype(F32) * halo_scale, x_ref[:, sl].astype(F32)], axis=0)
        w = cw_ref[:, sl]
        acc = cat * w[GDN_CONV - 1:GDN_CONV]
        for s in range(1, GDN_CONV):
            acc = acc + pltpu.roll(cat, s, axis=0) * w[GDN_CONV - 1 - s:GDN_CONV - s]
        y = _silu(acc[HALO:])
        part, head = divmod(j, N_HEADS)
        if part < 2:
            y = y * lax.rsqrt(jnp.sum(y * y, axis=-1, keepdims=True) + 1e-6)
        if part == 0:
            y = y * HEAD_DIM ** -0.5
        outs[part][:, LANE * head:LANE * (head + 1)] = y.astype(BF16)

    ab = ab_ref[...].astype(F32)
    z = ab + dtb_ref[...]
    softplus = jnp.maximum(z, 0.0) + jnp.log1p(jnp.exp(-jnp.abs(z)))
    g = -jnp.exp(alog_ref[...]) * softplus
    beta = 1.0 / (1.0 + jnp.exp(-ab))
    row = lax.broadcasted_iota(jnp.int32, g.shape, 0) % CHUNK
    s = 1
    while s < CHUNK:
        g = g + jnp.where(row >= s, pltpu.roll(g, s, axis=0), 0.0)
        s *= 2
    gt = g.T
    for h in range(N_HEADS):
        gcol_ref[h] = jnp.broadcast_to(g[:, h:h + 1], g.shape)
        bcol_ref[h] = jnp.broadcast_to(beta[:, N_HEADS + h:N_HEADS + h + 1], g.shape)
        grow_ref[h] = gt[h:h + 1, :]


def _gdn_prep(h2d, conv_w, alog, dtb, b, t):
    m = b * t
    tm = GDN_GROUP
    nt = t // tm
    return pl.pallas_call(
        functools.partial(_gdn_prep_kernel, nt),
        grid=(m // tm,),
        in_specs=[pl.BlockSpec((tm, 3 * GROUP_WIDTH), lambda i: (i, 0)),
                  pl.BlockSpec((HALO, 3 * GROUP_WIDTH), lambda i: (jnp.maximum(i * (tm // HALO) - 1, 0), 0)),
                  pl.BlockSpec((tm, LANE), lambda i: (i, COL_GDN_AB)),
                  pl.BlockSpec((GDN_CONV, 3 * GROUP_WIDTH), lambda i: (0, 0)),
                  pl.BlockSpec((1, LANE), lambda i: (0, 0)),
                  pl.BlockSpec((1, LANE), lambda i: (0, 0))],
        out_specs=[pl.BlockSpec((tm, GROUP_WIDTH), lambda i: (i, 0)),
                   pl.BlockSpec((tm, GROUP_WIDTH), lambda i: (i, 0)),
                   pl.BlockSpec((tm, GROUP_WIDTH), lambda i: (i, 0)),
                   pl.BlockSpec((None, N_HEADS, tm, LANE), lambda i: (i // nt, 0, i % nt, 0)),
                   pl.BlockSpec((None, N_HEADS, tm, LANE), lambda i: (i // nt, 0, i % nt, 0)),
                   pl.BlockSpec((None, N_HEADS, 1, tm), lambda i: (i // nt, 0, 0, i % nt))],
        out_shape=[jax.ShapeDtypeStruct((m, GROUP_WIDTH), BF16),
                   jax.ShapeDtypeStruct((m, GROUP_WIDTH), BF16),
                   jax.ShapeDtypeStruct((m, GROUP_WIDTH), BF16),
                   jax.ShapeDtypeStruct((b, N_HEADS, t, LANE), F32),
                   jax.ShapeDtypeStruct((b, N_HEADS, t, LANE), F32),
                   jax.ShapeDtypeStruct((b, N_HEADS, 1, t), F32)],
        compiler_params=_params("arbitrary"),
        name="gdn_prep",
    )(h2d, h2d, h2d, conv_w, alog, dtb)


def _gdn_kernel(q_ref, k_ref, v_ref, gc_ref, bc_ref, gr_ref, gate_ref, nw_ref, o_ref, state_ref, vnew_ref):
    n = pl.program_id(2)

    @pl.when(n == 0)
    def _():
        state_ref[...] = jnp.zeros_like(state_ref)

    g_sz = GDN_GROUP
    n_chunks = g_sz // CHUNK
    q16 = q_ref[...]
    k16 = k_ref[...]
    kf = k16.astype(F32)
    gc = gc_ref[...]
    beta = bc_ref[...]
    eg = jnp.exp(gc)
    kbeta = kf * beta

    ri = lax.broadcasted_iota(jnp.int32, (g_sz, g_sz), 0)
    ci = lax.broadcasted_iota(jnp.int32, (g_sz, g_sz), 1)
    dif = jnp.where((ri // CHUNK) == (ci // CHUNK), ri - ci, -1)
    incl = dif >= 0
    e = jnp.concatenate([gc, gc], axis=1) - gr_ref[...]
    decay = jnp.where(incl, jnp.exp(jnp.where(incl, e, 0.0)), 0.0)

    mpow = jnp.where(dif > 0, -(_dot_nt(kbeta.astype(BF16), k16) * decay), 0.0)
    inv = jnp.where(dif == 0, 1.0, 0.0) + mpow
    mpow16 = mpow.astype(BF16)
    for _ in range(int(math.log2(CHUNK)) - 1):
        mpow16 = _dot(mpow16, mpow16).astype(BF16)
        inv = inv + _dot(inv.astype(BF16), mpow16)

    rhs = jnp.concatenate([v_ref[...].astype(F32) * beta, kbeta * eg], axis=1).astype(BF16)
    uw = _dot(inv.astype(BF16), rhs)
    u = uw[:, :HEAD_DIM]
    w = uw[:, HEAD_DIM:]
    qk = (_dot_nt(q16, k16) * decay).astype(BF16)
    qe = (q16.astype(F32) * eg).astype(BF16)
    g_last = jnp.concatenate(
        [jnp.broadcast_to(gc[CHUNK * (c + 1) - 1:CHUNK * (c + 1), :], (CHUNK, LANE)) for c in range(n_chunks)], axis=0)
    kdec_t = (kf * jnp.exp(g_last - gc)).T.astype(BF16)
    col_chunk = lax.broadcasted_iota(jnp.int32, kdec_t.shape, 1) // CHUNK

    vnew_ref[...] = u
    state = state_ref[...]
    outs = []
    for c in range(n_chunks):
        rows = slice(CHUNK * c, CHUNK * (c + 1))
        s16 = state.astype(BF16)
        vnew_ref[rows, :] = u[rows] - _dot(w[rows].astype(BF16), s16)
        vfull = vnew_ref[...].astype(BF16)
        outs.append(_dot(qe[rows], s16) + _dot(qk[rows], vfull))
        state = (state * jnp.exp(gc[CHUNK * (c + 1) - 1:CHUNK * (c + 1), :])
                 + _dot(jnp.where(col_chunk == c, kdec_t, jnp.zeros_like(kdec_t)), vfull))
    state_ref[...] = state
    o = jnp.concatenate(outs, axis=0)
    o = o * lax.rsqrt(jnp.mean(o * o, axis=-1, keepdims=True) + 1e-6) * nw_ref[...]
    o_ref[...] = (o * _silu(gate_ref[...].astype(F32))).astype(o_ref.dtype)


def _gated_delta_net(qn, kn, vv, gcol, bcol, grow, h2d, norm_w, b, t):
    m = b * t
    ng = t // GDN_GROUP
    blk = (GDN_GROUP, HEAD_DIM)
    tok = lambda bi, h, n: (bi * ng + n, h)
    return pl.pallas_call(
        _gdn_kernel,
        grid=(b, N_HEADS, ng),
        in_specs=[pl.BlockSpec(blk, tok), pl.BlockSpec(blk, tok), pl.BlockSpec(blk, tok),
                  pl.BlockSpec((None, None, GDN_GROUP, LANE), lambda bi, h, n: (bi, h, n, 0)),
                  pl.BlockSpec((None, None, GDN_GROUP, LANE), lambda bi, h, n: (bi, h, n, 0)),
                  pl.BlockSpec((None, None, 1, GDN_GROUP), lambda bi, h, n: (bi, h, 0, n)),
                  pl.BlockSpec(blk, lambda bi, h, n: (bi * ng + n, COL_GATE + 8 + h)),
                  pl.BlockSpec((1, HEAD_DIM), lambda bi, h, n: (0, 0))],
        out_specs=pl.BlockSpec(blk, tok),
        out_shape=jax.ShapeDtypeStruct((m, GROUP_WIDTH), BF16),
        scratch_shapes=[pltpu.VMEM((HEAD_DIM, HEAD_DIM), F32), pltpu.VMEM((GDN_GROUP, HEAD_DIM), F32)],
        compiler_params=_params("arbitrary", "arbitrary", "arbitrary"),
        name="gated_delta_net",
    )(qn, kn, vv, gcol, bcol, grow, h2d, norm_w)


def _out_proj_kernel(n_col_tiles, oa_ref, ob_ref, oc_ref, od_ref, w_ref, x_ref, g_ref, b_ref,
                     y_ref, y16_ref, acc_ref):
    j = pl.program_id(1)
    o = jnp.concatenate([oa_ref[...], ob_ref[...], oc_ref[...], od_ref[...]], axis=1)
    acc_ref[j] = _dot(o, w_ref[...]) + DEEPNORM_ALPHA * x_ref[...]

    @pl.when(j == n_col_tiles - 1)
    def _():
        tn = acc_ref.shape[2]
        total = acc_ref[0].sum(axis=-1, keepdims=True)
        for jj in range(1, n_col_tiles):
            total = total + acc_ref[jj].sum(axis=-1, keepdims=True)
        mean = total * (1.0 / D_MODEL)
        sq = jnp.zeros_like(mean)
        for jj in range(n_col_tiles):
            xc = acc_ref[jj] - mean
            sq = sq + (xc * xc).sum(axis=-1, keepdims=True)
        inv = lax.rsqrt(sq * (1.0 / D_MODEL) + 1e-5)
        for jj in range(n_col_tiles):
            sl = slice(tn * jj, tn * (jj + 1))
            y = (acc_ref[jj] - mean) * inv * g_ref[:, sl] + b_ref[:, sl]
            y_ref[:, sl] = y
            y16_ref[:, sl] = y.astype(BF16)


def _out_proj_ln(oa, ob, oc, od, w16, x2d, gain, bias):
    m = x2d.shape[0]
    tm, tn = 256, 1024
    nj = D_MODEL // tn
    grp = pl.BlockSpec((tm, GROUP_WIDTH), lambda i, j: (i, 0))
    row = pl.BlockSpec((tm, D_MODEL), lambda i, j: (i, 0))
    vec = pl.BlockSpec((1, D_MODEL), lambda i, j: (0, 0))
    return pl.pallas_call(
        functools.partial(_out_proj_kernel, nj),
        grid=(m // tm, nj),
        in_specs=[grp, grp, grp, grp,
                  pl.BlockSpec((D_MODEL, tn), lambda i, j: (0, j)),
                  pl.BlockSpec((tm, tn), lambda i, j: (i, j)),
                  vec, vec],
        out_specs=[row, row],
        out_shape=[jax.ShapeDtypeStruct((m, D_MODEL), F32), jax.ShapeDtypeStruct((m, D_MODEL), BF16)],
        scratch_shapes=[pltpu.VMEM((nj, tm, tn), F32)],
        compiler_params=_params("arbitrary", "arbitrary"),
        name="out_proj_layernorm",
    )(oa, ob, oc, od, w16, x2d, gain, bias)


def _rotate_half_cols(w):
    half = MLA_ROPE // 2
    return jnp.concatenate([-w[:, half:], w[:, :half]], axis=1)


def _layout_w_in(w):
    (a_q, a_k, a_v, b_qkv, b_a, b_b, c_dq, c_dkv, d_q, d_k, d_v, gate) = jnp.split(w, _split_points(), axis=1)
    ab = jnp.concatenate([b_a, b_b, jnp.zeros((w.shape[0], LANE - 2 * N_HEADS), w.dtype)], axis=1)
    k_rot = _rotate_half_cols(c_dkv[:, MLA_KV_RANK:])
    out = jnp.concatenate([b_qkv, c_dq, c_dkv, k_rot, ab, a_q, a_k, a_v, d_q, d_k, d_v, gate], axis=1)
    return out.astype(BF16)


def _split_points():
    pts, s = [], 0
    for n in IN_SIZES[:-1]:
        s += n
        pts.append(s)
    return pts


def _layout_w_uq(w):
    w = w.reshape(MLA_Q_RANK, N_HEADS, MLA_NOPE + MLA_ROPE)
    zeros = jnp.zeros((MLA_Q_RANK, N_HEADS, LANE - MLA_ROPE), w.dtype)
    rope = w[:, :, MLA_NOPE:]
    rot = jnp.concatenate([-rope[:, :, MLA_ROPE // 2:], rope[:, :, :MLA_ROPE // 2]], axis=2)
    w1 = jnp.concatenate([w, zeros], axis=2).reshape(MLA_Q_RANK, N_HEADS * 2 * LANE)
    w2 = jnp.concatenate([rot, zeros], axis=2).reshape(MLA_Q_RANK, N_HEADS * LANE)
    return w1.astype(BF16), w2.astype(BF16)


def _layout_w_ukv(w):
    w = w.reshape(MLA_KV_RANK, N_HEADS, MLA_NOPE + HEAD_DIM)
    wk = w[:, :, :MLA_NOPE].reshape(MLA_KV_RANK, GROUP_WIDTH)
    wv = w[:, :, MLA_NOPE:].reshape(MLA_KV_RANK, GROUP_WIDTH)
    return wk.astype(BF16), wv.astype(BF16)


def _rope_tables(t):
    half = MLA_ROPE // 2
    inv = ROPE_THETA ** (-jnp.arange(half, dtype=F32) / half)
    ang = jnp.arange(t).astype(F32)[:, None] * inv[None, :]
    zeros = jnp.zeros((t, LANE - MLA_ROPE), F32)
    cos_t = jnp.concatenate([jnp.cos(ang), jnp.cos(ang), zeros], axis=1)
    sin_t = jnp.concatenate([jnp.sin(ang), jnp.sin(ang), zeros], axis=1)
    return cos_t, sin_t


def _pad_lanes(v):
    return jnp.concatenate([v.astype(F32), jnp.zeros((LANE - v.shape[0],), F32)])[None, :]


def _layer(x2d, x16, layer_idx, b, t, w_in, diff_lambda, diff_norm, gdn_conv, gdn_a_log, gdn_dt_bias, gdn_norm,
           mla_q_norm, mla_w_uq, mla_kv_norm, mla_w_ukv, rel_bias, w_out, ln_gain, ln_bias, tables):
    h2d = _in_proj(x16, _layout_w_in(w_in))

    lam_init = 0.8 - 0.6 * math.exp(-0.3 * layer_idx)
    slopes = 2.0 ** (-8.0 * jnp.arange(1, N_HEADS + 1, dtype=F32) / N_HEADS)
    slopes = jnp.broadcast_to(slopes[:, None, None], (N_HEADS, 1, LANE))
    lam_p = jnp.concatenate([diff_lambda.astype(F32), jnp.zeros((4, LANE - DIFF_HALF), F32)], axis=1)
    o_a = _diff_attention(h2d, slopes, lam_p, diff_norm.astype(F32)[None, :], lam_init, b, t)

    qn, kn, vv, gcol, bcol, grow = _gdn_prep(h2d, gdn_conv.astype(F32), _pad_lanes(gdn_a_log),
                                              _pad_lanes(gdn_dt_bias), b, t)
    o_b = _gated_delta_net(qn, kn, vv, gcol, bcol, grow, h2d, gdn_norm.astype(F32)[None, :], b, t)

    w1, w2 = _layout_w_uq(mla_w_uq)
    wk, wv = _layout_w_ukv(mla_w_ukv)
    cos_t, sin_t = tables
    qf, kf, vc = _mla_prep(h2d, mla_q_norm.astype(F32)[None, :], mla_kv_norm.astype(F32)[None, :],
                           w1, w2, wk, wv, cos_t, sin_t, b, t)
    o_c = _mla_attention(qf, kf, vc, h2d, b, t)

    o_d = _band_attention(h2d, _band_bias_tiles(rel_bias), b, t)

    return _out_proj_ln(o_a, o_b, o_c, o_d, w_out.astype(BF16), x2d,
                        ln_gain.astype(F32)[None, :], ln_bias.astype(F32)[None, :])


def kernel(x, w_in, diff_lambda, diff_norm, gdn_conv, gdn_a_log, gdn_dt_bias, gdn_norm, mla_q_norm, mla_w_uq,
           mla_kv_norm, mla_w_ukv, rel_bias, w_out, ln_gain, ln_bias):
    b, t, d = x.shape
    assert d == D_MODEL and t % GDN_GROUP == 0 and (b * t) % 512 == 0
    tables = _rope_tables(t)
    x2d = x.reshape(b * t, d)
    x16 = x2d.astype(BF16)
    for l in range(DEPTH):
        x2d, x16 = _layer(x2d, x16, l, b, t, w_in[l], diff_lambda[l], diff_norm[l], gdn_conv[l], gdn_a_log[l],
                          gdn_dt_bias[l], gdn_norm[l], mla_q_norm[l], mla_w_uq[l], mla_kv_norm[l],
                          mla_w_ukv[l], rel_bias[l], w_out[l], ln_gain[l], ln_bias[l], tables)
    return x2d.reshape(b, t, d)
```

```python
import functools
import math

import jax
import jax.numpy as jnp
from jax import lax
from jax.experimental import pallas as pl
from jax.experimental.pallas import tpu as pltpu

F32 = jnp.float32
BF16 = jnp.bfloat16

D_MODEL = 4096
DEPTH = 2
CHUNK = 64
N_HEADS = 8
HEAD_DIM = 128
GROUP_WIDTH = N_HEADS * HEAD_DIM
DIFF_HALF = HEAD_DIM // 2
GDN_CONV = 4
MLA_Q_RANK = 768
MLA_KV_RANK = 256
MLA_NOPE = 128
MLA_ROPE = 64
ROPE_THETA = 10000.0
BAND_CHUNKS = 9
REL_CLIP = 128
DEEPNORM_ALPHA = (2 * DEPTH) ** 0.25
IN_SIZES = (GROUP_WIDTH, GROUP_WIDTH, GROUP_WIDTH, 3 * GROUP_WIDTH, N_HEADS, N_HEADS,
            MLA_Q_RANK, MLA_KV_RANK + MLA_ROPE, GROUP_WIDTH, GROUP_WIDTH, GROUP_WIDTH, D_MODEL)

LANE = 128
COL_GDN_QKV = 0
COL_MLA_Q = 24
COL_MLA_KV = 30
COL_GDN_AB = 33
COL_DIFF_Q = 34
COL_DIFF_K = 42
COL_DIFF_V = 50
COL_BAND_Q = 58
COL_BAND_K = 66
COL_BAND_V = 74
COL_GATE = 82
N_COL_BLOCKS = 114
N_COLS = N_COL_BLOCKS * LANE

ATT_BLOCK = 256
GDN_GROUP = 256
NEG_BIG = -1e30
VMEM_LIMIT = 48 * 1024 * 1024


def _params(*sem):
    return pltpu.CompilerParams(dimension_semantics=sem, vmem_limit_bytes=VMEM_LIMIT)


def _dot(a, b):
    return jnp.dot(a, b, preferred_element_type=F32)


def _dot_nt(a, b):
    return lax.dot_general(a, b, (((1,), (1,)), ((), ())), preferred_element_type=F32)


def _silu(x):
    return x * (1.0 / (1.0 + jnp.exp(-x)))


def _matmul_kernel(x_ref, w_ref, o_ref):
    o_ref[...] = _dot(x_ref[...], w_ref[...]).astype(o_ref.dtype)


def _in_proj(xb, wb):
    m = xb.shape[0]
    tm, tn = 512, 768
    return pl.pallas_call(
        _matmul_kernel,
        grid=(N_COLS // tn, m // tm),
        in_specs=[pl.BlockSpec((tm, D_MODEL), lambda j, i: (i, 0)),
                  pl.BlockSpec((D_MODEL, tn), lambda j, i: (0, j))],
        out_specs=pl.BlockSpec((tm, tn), lambda j, i: (i, j)),
        out_shape=jax.ShapeDtypeStruct((m, N_COLS), BF16),
        compiler_params=_params("arbitrary", "arbitrary"),
        name="in_proj",
    )(xb, wb)


def _softmax_stats(m, l, t, shift):
    m_new = jnp.maximum(m, jnp.max(t, axis=-1, keepdims=True) + shift)
    alpha = jnp.exp(m - m_new)
    p = jnp.exp(t - (m_new - shift))
    return m_new, alpha * l + jnp.sum(p, axis=-1, keepdims=True), alpha, p.astype(BF16)


def _attn_sweep(qs, k_ref, v_ref, lo, i, off_scores, diag_scores):
    tq = qs[0].shape[0]

    def rows(ref, kb):
        return ref[pl.ds(pl.multiple_of(kb * ATT_BLOCK, ATT_BLOCK), ATT_BLOCK), :]

    def body(kb, carry):
        k_next = rows(k_ref, kb + 1)
        v_prev = rows(v_ref, jnp.maximum(kb - 1, lo))
        new = []
        for (m, l, acc, s, p_prev, a_prev), q in zip(carry, qs):
            s_next = _dot_nt(q, k_next)
            acc = a_prev * acc + _dot(p_prev, v_prev)
            m, l, alpha, p = _softmax_stats(m, l, *off_scores(s, kb))
            new.append((m, l, acc, s_next, p, alpha))
        return tuple(new)

    k_lo = rows(k_ref, lo)
    init = tuple((jnp.full((tq, 1), NEG_BIG, F32), jnp.zeros((tq, 1), F32), jnp.zeros((tq, HEAD_DIM), F32),
                  _dot_nt(q, k_lo), jnp.zeros((tq, ATT_BLOCK), BF16), jnp.ones((tq, 1), F32)) for q in qs)
    carry = lax.fori_loop(lo, i, body, init)
    v_prev = rows(v_ref, jnp.maximum(i - 1, lo))
    v_diag = rows(v_ref, i)
    out = []
    for m, l, acc, s, p_prev, a_prev in carry:
        acc = a_prev * acc + _dot(p_prev, v_prev)
        m, l, alpha, p = _softmax_stats(m, l, diag_scores(s), 0.0)
        out.append((alpha * acc + _dot(p, v_diag), l))
    return out


def _block_iotas():
    r = lax.broadcasted_iota(jnp.int32, (ATT_BLOCK, ATT_BLOCK), 0)
    c = lax.broadcasted_iota(jnp.int32, (ATT_BLOCK, ATT_BLOCK), 1)
    return r, c


def _diff_attn_kernel(lam_init, q_ref, k_ref, v_ref, gate_ref, slope_ref, lam_ref, nw_ref, o_ref):
    i = pl.program_id(2)
    q = q_ref[...] * (DIFF_HALF ** -0.5)
    lane = lax.broadcasted_iota(jnp.int32, q.shape, 1)
    q1 = jnp.where(lane < DIFF_HALF, q, jnp.zeros_like(q))
    q2 = jnp.where(lane >= DIFF_HALF, q, jnp.zeros_like(q))
    r, c = _block_iotas()
    rc = (r - c).astype(F32)
    slope = slope_ref[:, 0:1]
    bias_off = -slope * rc
    bias_diag = jnp.where((c // CHUNK) <= (r // CHUNK), -slope * jnp.abs(rc), NEG_BIG)

    def off_scores(s, kb):
        return s + bias_off, -slope * ((i - kb) * ATT_BLOCK).astype(F32)

    def diag_scores(s):
        return s + bias_diag

    (acc1, l1), (acc2, l2) = _attn_sweep((q1, q2), k_ref, v_ref, 0, i, off_scores, diag_scores)
    lp = lam_ref[...]
    lam = (jnp.exp(jnp.sum(lp[0:1] * lp[1:2], axis=-1, keepdims=True))
           - jnp.exp(jnp.sum(lp[2:3] * lp[3:4], axis=-1, keepdims=True)) + lam_init)
    o = acc1 / l1 - lam * (acc2 / l2)
    o = o * lax.rsqrt(jnp.mean(o * o, axis=-1, keepdims=True) + 1e-6) * nw_ref[...]
    o = o * (1.0 - lam_init)
    o_ref[...] = (o * _silu(gate_ref[...].astype(F32))).astype(o_ref.dtype)


def _diff_attention(h2d, slopes, lam_p, norm_w, lam_init, b, t):
    nq = t // ATT_BLOCK
    m = b * t
    blk = (ATT_BLOCK, HEAD_DIM)
    return pl.pallas_call(
        functools.partial(_diff_attn_kernel, lam_init),
        grid=(b, N_HEADS, nq),
        in_specs=[pl.BlockSpec(blk, lambda bi, h, i: (bi * nq + i, COL_DIFF_Q + h)),
                  pl.BlockSpec((t, HEAD_DIM), lambda bi, h, i: (bi, COL_DIFF_K + h)),
                  pl.BlockSpec((t, HEAD_DIM), lambda bi, h, i: (bi, COL_DIFF_V + h)),
                  pl.BlockSpec(blk, lambda bi, h, i: (bi * nq + i, COL_GATE + h)),
                  pl.BlockSpec((None, 1, LANE), lambda bi, h, i: (h, 0, 0)),
                  pl.BlockSpec((4, LANE), lambda bi, h, i: (0, 0)),
                  pl.BlockSpec((1, HEAD_DIM), lambda bi, h, i: (0, 0))],
        out_specs=pl.BlockSpec(blk, lambda bi, h, i: (bi * nq + i, h)),
        out_shape=jax.ShapeDtypeStruct((m, GROUP_WIDTH), BF16),
        compiler_params=_params("arbitrary", "arbitrary", "arbitrary"),
        name="diff_attention",
    )(h2d, h2d, h2d, h2d, slopes, lam_p, norm_w)


def _mla_attn_kernel(q_ref, k_ref, v_ref, gate_ref, o_ref):
    i = pl.program_id(2)
    r, c = _block_iotas()
    mask_diag = jnp.where((c // CHUNK) <= (r // CHUNK), 0.0, NEG_BIG)
    scale = (MLA_NOPE + MLA_ROPE) ** -0.5

    def off_scores(s, kb):
        return s * scale, 0.0

    def diag_scores(s):
        return s * scale + mask_diag

    ((acc, l),) = _attn_sweep((q_ref[...],), k_ref, v_ref, 0, i, off_scores, diag_scores)
    o_ref[...] = ((acc / l) * _silu(gate_ref[...].astype(F32))).astype(o_ref.dtype)


def _mla_attention(qf, kf, vv, h2d, b, t):
    nq = t // ATT_BLOCK
    m = b * t
    return pl.pallas_call(
        _mla_attn_kernel,
        grid=(b, N_HEADS, nq),
        in_specs=[pl.BlockSpec((ATT_BLOCK, 2 * LANE), lambda bi, h, i: (bi * nq + i, h)),
                  pl.BlockSpec((t, 2 * LANE), lambda bi, h, i: (bi, h)),
                  pl.BlockSpec((t, HEAD_DIM), lambda bi, h, i: (bi, h)),
                  pl.BlockSpec((ATT_BLOCK, HEAD_DIM), lambda bi, h, i: (bi * nq + i, COL_GATE + 16 + h))],
        out_specs=pl.BlockSpec((ATT_BLOCK, HEAD_DIM), lambda bi, h, i: (bi * nq + i, h)),
        out_shape=jax.ShapeDtypeStruct((m, GROUP_WIDTH), BF16),
        compiler_params=_params("arbitrary", "arbitrary", "arbitrary"),
        name="mla_attention",
    )(qf, kf, vv, h2d)


BAND_KEY_BLOCKS = (BAND_CHUNKS - 1) * CHUNK // ATT_BLOCK + 1


def _band_attn_kernel(q_ref, k_ref, v_ref, gate_ref, bias_ref, o_ref):
    i = pl.program_id(2)
    scale = HEAD_DIM ** -0.5

    def off_scores(s, kb):
        return s * scale + bias_ref[kb - i + (BAND_KEY_BLOCKS - 1)], 0.0

    def diag_scores(s):
        return s * scale + bias_ref[BAND_KEY_BLOCKS - 1]

    lo = jnp.maximum(i - (BAND_KEY_BLOCKS - 1), 0)
    ((acc, l),) = _attn_sweep((q_ref[...],), k_ref, v_ref, lo, i, off_scores, diag_scores)
    o_ref[...] = ((acc / l) * _silu(gate_ref[...].astype(F32))).astype(o_ref.dtype)


def _band_attention(h2d, bias_tiles, b, t):
    nq = t // ATT_BLOCK
    m = b * t
    blk = (ATT_BLOCK, HEAD_DIM)
    return pl.pallas_call(
        _band_attn_kernel,
        grid=(b, N_HEADS, nq),
        in_specs=[pl.BlockSpec(blk, lambda bi, h, i: (bi * nq + i, COL_BAND_Q + h)),
                  pl.BlockSpec((t, HEAD_DIM), lambda bi, h, i: (bi, COL_BAND_K + h)),
                  pl.BlockSpec((t, HEAD_DIM), lambda bi, h, i: (bi, COL_BAND_V + h)),
                  pl.BlockSpec(blk, lambda bi, h, i: (bi * nq + i, COL_GATE + 24 + h)),
                  pl.BlockSpec((None, BAND_KEY_BLOCKS, ATT_BLOCK, ATT_BLOCK), lambda bi, h, i: (h, 0, 0, 0))],
        out_specs=pl.BlockSpec(blk, lambda bi, h, i: (bi * nq + i, h)),
        out_shape=jax.ShapeDtypeStruct((m, GROUP_WIDTH), BF16),
        compiler_params=_params("arbitrary", "arbitrary", "arbitrary"),
        name="band_attention",
    )(h2d, h2d, h2d, h2d, bias_tiles)


def _band_bias_tiles(rel_bias):
    n = ATT_BLOCK
    back = (BAND_KEY_BLOCKS - 1) - jnp.arange(BAND_KEY_BLOCKS)
    j = jnp.arange(2 * n)
    c_minus_r = jnp.where(j < n, j, j - 2 * n)
    rel = back[:, None] * n - c_minus_r[None, :]
    vals = rel_bias.astype(F32)[:, jnp.clip(rel, -REL_CLIP, REL_CLIP) + REL_CLIP]
    flat = jnp.tile(vals, (1, 1, n))[:, :, :n * (2 * n - 1)]
    tiles = flat.reshape(N_HEADS, BAND_KEY_BLOCKS, n, 2 * n - 1)[:, :, :, :n]
    r = jnp.arange(n)[None, :, None]
    c = jnp.arange(n)[None, None, :]
    chunk_back = back[:, None, None] * (n // CHUNK) + r // CHUNK - c // CHUNK
    visible = (chunk_back >= 0) & (chunk_back < BAND_CHUNKS)
    return jnp.where(visible[None], tiles, NEG_BIG)


def _mla_prep_kernel(cdq_ref, ckv_ref, qg_ref, kvg_ref, w1_ref, w2_ref, wk_ref, wv_ref, ct_ref, st_ref,
                     q_out, k_out, v_out):
    ct = ct_ref[...]
    st = st_ref[...]
    cq = cdq_ref[...].astype(F32)
    nq = cq * lax.rsqrt(jnp.mean(cq * cq, axis=-1, keepdims=True) + 1e-6) * qg_ref[...]
    nq = nq.astype(BF16)
    q1 = _dot(nq, w1_ref[...])
    q2 = _dot(nq, w2_ref[...])
    ckv = ckv_ref[...].astype(F32)
    lat = ckv[:, :MLA_KV_RANK]
    nkv = lat * lax.rsqrt(jnp.mean(lat * lat, axis=-1, keepdims=True) + 1e-6) * kvg_ref[...]
    nkv = nkv.astype(BF16)
    kn = _dot(nkv, wk_ref[...])
    v_out[...] = _dot(nkv, wv_ref[...]).astype(v_out.dtype)
    kr = ckv[:, MLA_KV_RANK:]
    kr = (kr * ct + pltpu.roll(kr, MLA_ROPE, axis=1) * st).astype(k_out.dtype)
    for h in range(N_HEADS):
        a, bq = 2 * LANE * h, LANE * h
        q_out[:, a:a + LANE] = q1[:, a:a + LANE].astype(q_out.dtype)
        q_out[:, a + LANE:a + 2 * LANE] = (q1[:, a + LANE:a + 2 * LANE] * ct
                                           + q2[:, bq:bq + LANE] * st).astype(q_out.dtype)
        k_out[:, a:a + LANE] = kn[:, bq:bq + LANE].astype(k_out.dtype)
        k_out[:, a + LANE:a + 2 * LANE] = kr


def _mla_prep(h2d, q_gain, kv_gain, w1, w2, wk, wv, cos_t, sin_t, b, t):
    m = b * t
    tm = 256
    nt = t // tm
    full = lambda shape: pl.BlockSpec(shape, lambda i: (0, 0))
    return pl.pallas_call(
        _mla_prep_kernel,
        grid=(m // tm,),
        in_specs=[pl.BlockSpec((tm, MLA_Q_RANK), lambda i: (i, COL_MLA_Q * LANE // MLA_Q_RANK)),
                  pl.BlockSpec((tm, 3 * LANE), lambda i: (i, COL_MLA_KV // 3)),
                  full((1, MLA_Q_RANK)), full((1, MLA_KV_RANK)),
                  full(w1.shape), full(w2.shape), full(wk.shape), full(wv.shape),
                  pl.BlockSpec((tm, LANE), lambda i: (i % nt, 0)),
                  pl.BlockSpec((tm, LANE), lambda i: (i % nt, 0))],
        out_specs=[pl.BlockSpec((tm, 2 * GROUP_WIDTH), lambda i: (i, 0)),
                   pl.BlockSpec((tm, 2 * GROUP_WIDTH), lambda i: (i, 0)),
                   pl.BlockSpec((tm, GROUP_WIDTH), lambda i: (i, 0))],
        out_shape=[jax.ShapeDtypeStruct((m, 2 * GROUP_WIDTH), BF16),
                   jax.ShapeDtypeStruct((m, 2 * GROUP_WIDTH), BF16),
                   jax.ShapeDtypeStruct((m, GROUP_WIDTH), BF16)],
        compiler_params=_params("arbitrary"),
        name="mla_prep",
    )(h2d, h2d, q_gain, kv_gain, w1, w2, wk, wv, cos_t, sin_t)


HALO = 16


def _gdn_prep_kernel(blocks_per_seq, x_ref, halo_ref, ab_ref, cw_ref, alog_ref, dtb_ref,
                     qn_ref, kn_ref, v_ref, gcol_ref, bcol_ref, grow_ref):
    i = pl.program_id(0)
    halo_scale = jnp.where(i % blocks_per_seq == 0, 0.0, 1.0)
    outs = (qn_ref, kn_ref, v_ref)
    for j in range(3 * N_HEADS):
        sl = slice(LANE * j, LANE * (j + 1))
        cat = jnp.concatenate([halo_ref[:, sl].astype(F32) * halo_scale, x_ref[:, sl].astype(F32)], axis=0)
        w = cw_ref[:, sl]
        acc = cat * w[GDN_CONV - 1:GDN_CONV]
        for s in range(1, GDN_CONV):
            acc = acc + pltpu.roll(cat, s, axis=0) * w[GDN_CONV - 1 - s:GDN_CONV - s]
        y = _silu(acc[HALO:])
        part, head = divmod(j, N_HEADS)
        if part < 2:
            y = y * lax.rsqrt(jnp.sum(y * y, axis=-1, keepdims=True) + 1e-6)
        if part == 0:
            y = y * HEAD_DIM ** -0.5
        outs[part][:, LANE * head:LANE * (head + 1)] = y.astype(BF16)

    ab = ab_ref[...].astype(F32)
    z = ab + dtb_ref[...]
    softplus = jnp.maximum(z, 0.0) + jnp.log1p(jnp.exp(-jnp.abs(z)))
    g = -jnp.exp(alog_ref[...]) * softplus
    beta = 1.0 / (1.0 + jnp.exp(-ab))
    row = lax.broadcasted_iota(jnp.int32, g.shape, 0) % CHUNK
    s = 1
    while s < CHUNK:
        g = g + jnp.where(row >= s, pltpu.roll(g, s, axis=0), 0.0)
        s *= 2
    gt = g.T
    for h in range(N_HEADS):
        gcol_ref[h] = jnp.broadcast_to(g[:, h:h + 1], g.shape)
        bcol_ref[h] = jnp.broadcast_to(beta[:, N_HEADS + h:N_HEADS + h + 1], g.shape)
        grow_ref[h] = gt[h:h + 1, :]


def _gdn_prep(h2d, conv_w, alog, dtb, b, t):
    m = b * t
    tm = GDN_GROUP
    nt = t // tm
    return pl.pallas_call(
        functools.partial(_gdn_prep_kernel, nt),
        grid=(m // tm,),
        in_specs=[pl.BlockSpec((tm, 3 * GROUP_WIDTH), lambda i: (i, 0)),
                  pl.BlockSpec((HALO, 3 * GROUP_WIDTH), lambda i: (jnp.maximum(i * (tm // HALO) - 1, 0), 0)),
                  pl.BlockSpec((tm, LANE), lambda i: (i, COL_GDN_AB)),
                  pl.BlockSpec((GDN_CONV, 3 * GROUP_WIDTH), lambda i: (0, 0)),
                  pl.BlockSpec((1, LANE), lambda i: (0, 0)),
                  pl.BlockSpec((1, LANE), lambda i: (0, 0))],
        out_specs=[pl.BlockSpec((tm, GROUP_WIDTH), lambda i: (i, 0)),
                   pl.BlockSpec((tm, GROUP_WIDTH), lambda i: (i, 0)),
                   pl.BlockSpec((tm, GROUP_WIDTH), lambda i: (i, 0)),
                   pl.BlockSpec((None, N_HEADS, tm, LANE), lambda i: (i // nt, 0, i % nt, 0)),
                   pl.BlockSpec((None, N_HEADS, tm, LANE), lambda i: (i // nt, 0, i % nt, 0)),
                   pl.BlockSpec((None, N_HEADS, 1, tm), lambda i: (i // nt, 0, 0, i % nt))],
        out_shape=[jax.ShapeDtypeStruct((m, GROUP_WIDTH), BF16),
                   jax.ShapeDtypeStruct((m, GROUP_WIDTH), BF16),
                   jax.ShapeDtypeStruct((m, GROUP_WIDTH), BF16),
                   jax.ShapeDtypeStruct((b, N_HEADS, t, LANE), F32),
                   jax.ShapeDtypeStruct((b, N_HEADS, t, LANE), F32),
                   jax.ShapeDtypeStruct((b, N_HEADS, 1, t), F32)],
        compiler_params=_params("arbitrary"),
        name="gdn_prep",
    )(h2d, h2d, h2d, conv_w, alog, dtb)


GDN_HEADS_PER_STEP = 4


def _gdn_kernel(q_ref, k_ref, v_ref, gc_ref, bc_ref, gr_ref, gate_lo_ref, gate_hi_ref, nw_ref, o_ref,
                state_ref):
    n = pl.program_id(2)

    @pl.when(n == 0)
    def _():
        state_ref[...] = jnp.zeros_like(state_ref)

    g_sz = GDN_GROUP
    ri = lax.broadcasted_iota(jnp.int32, (g_sz, g_sz), 0)
    ci = lax.broadcasted_iota(jnp.int32, (g_sz, g_sz), 1)
    dif = jnp.where((ri // CHUNK) == (ci // CHUNK), ri - ci, -1)
    heads = range(GDN_HEADS_PER_STEP)
    cols = [slice(HEAD_DIM * hb, HEAD_DIM * (hb + 1)) for hb in heads]
    outs = _gdn_heads(dif, [q_ref[:, sl] for sl in cols], [k_ref[:, sl] for sl in cols],
                      [v_ref[:, sl] for sl in cols], [gc_ref[hb] for hb in heads], [bc_ref[hb] for hb in heads],
                      [gr_ref[hb] for hb in heads], state_ref)
    for hb in heads:
        o = outs[hb]
        o = o * lax.rsqrt(jnp.mean(o * o, axis=-1, keepdims=True) + 1e-6) * nw_ref[...]
        gate_ref = gate_lo_ref if hb < 2 else gate_hi_ref
        gate = gate_ref[:, HEAD_DIM * (hb % 2):HEAD_DIM * (hb % 2 + 1)]
        o_ref[:, cols[hb]] = (o * _silu(gate.astype(F32))).astype(o_ref.dtype)


def _gdn_heads(dif, q16, k16, v16, gc, beta, gr, state_ref):
    heads = range(len(q16))
    n_chunks = GDN_GROUP // CHUNK
    incl = dif >= 0
    kf = [k16[h].astype(F32) for h in heads]
    eg = [jnp.exp(gc[h]) for h in heads]
    kbeta = [kf[h] * beta[h] for h in heads]
    decay = [jnp.where(incl, jnp.exp(jnp.where(incl, jnp.concatenate([gc[h], gc[h]], axis=1) - gr[h], 0.0)), 0.0)
             for h in heads]

    kk = [_dot_nt(kbeta[h].astype(BF16), k16[h]) for h in heads]
    mpow = [jnp.where(dif > 0, -(kk[h] * decay[h]), 0.0) for h in heads]
    inv = [jnp.where(dif == 0, 1.0, 0.0) + mpow[h] for h in heads]
    mpow16 = [mpow[h].astype(BF16) for h in heads]
    for _ in range(int(math.log2(CHUNK)) - 1):
        mpow16 = [_dot(mpow16[h], mpow16[h]).astype(BF16) for h in heads]
        inv = [inv[h] + _dot(inv[h].astype(BF16), mpow16[h]) for h in heads]

    rhs = [jnp.concatenate([v16[h].astype(F32) * beta[h], kbeta[h] * eg[h]], axis=1).astype(BF16) for h in heads]
    uw = [_dot(inv[h].astype(BF16), rhs[h]) for h in heads]
    u = [uw[h][:, :HEAD_DIM] for h in heads]
    w = [uw[h][:, HEAD_DIM:].astype(BF16) for h in heads]
    qk = [(_dot_nt(q16[h], k16[h]) * decay[h]).astype(BF16) for h in heads]
    qe = [(q16[h].astype(F32) * eg[h]).astype(BF16) for h in heads]
    g_last_rows = [[gc[h][CHUNK * (c + 1) - 1:CHUNK * (c + 1), :] for c in range(n_chunks)] for h in heads]
    kdec_t = [(kf[h] * jnp.exp(jnp.concatenate([jnp.broadcast_to(g, (CHUNK, LANE)) for g in g_last_rows[h]], axis=0)
                               - gc[h])).T.astype(BF16) for h in heads]
    col_chunk = lax.broadcasted_iota(jnp.int32, kdec_t[0].shape, 1) // CHUNK

    vparts = [[u[h][CHUNK * c:CHUNK * (c + 1)] for c in range(n_chunks)] for h in heads]
    state = [state_ref[h] for h in heads]
    outs = [[] for _ in heads]
    for c in range(n_chunks):
        rows = slice(CHUNK * c, CHUNK * (c + 1))
        s16 = [state[h].astype(BF16) for h in heads]
        ws = [_dot(w[h][rows], s16[h]) for h in heads]
        for h in heads:
            vparts[h][c] = u[h][rows] - ws[h]
        vfull = [jnp.concatenate(vparts[h], axis=0).astype(BF16) for h in heads]
        upd = [_dot(jnp.where(col_chunk == c, kdec_t[h], jnp.zeros_like(kdec_t[h])), vfull[h]) for h in heads]
        state = [state[h] * jnp.exp(g_last_rows[h][c]) + upd[h] for h in heads]
        for h in heads:
            outs[h].append(_dot(qe[h][rows], s16[h]) + _dot(qk[h][rows], vfull[h]))
    for h in heads:
        state_ref[h] = state[h]
    return [jnp.concatenate(outs[h], axis=0) for h in heads]


def _gated_delta_net(qn, kn, vv, gcol, bcol, grow, h2d, norm_w, b, t):
    m = b * t
    ng = t // GDN_GROUP
    hb = GDN_HEADS_PER_STEP
    blk = (GDN_GROUP, HEAD_DIM * hb)
    tok = lambda bi, h, n: (bi * ng + n, h)
    return pl.pallas_call(
        _gdn_kernel,
        grid=(b, N_HEADS // hb, ng),
        in_specs=[pl.BlockSpec(blk, tok), pl.BlockSpec(blk, tok), pl.BlockSpec(blk, tok),
                  pl.BlockSpec((None, hb, GDN_GROUP, LANE), lambda bi, h, n: (bi, h, n, 0)),
                  pl.BlockSpec((None, hb, GDN_GROUP, LANE), lambda bi, h, n: (bi, h, n, 0)),
                  pl.BlockSpec((None, hb, 1, GDN_GROUP), lambda bi, h, n: (bi, h, 0, n)),
                  pl.BlockSpec((GDN_GROUP, 2 * HEAD_DIM), lambda bi, h, n: (bi * ng + n, (COL_GATE + 8) // 2 + 2 * h)),
                  pl.BlockSpec((GDN_GROUP, 2 * HEAD_DIM),
                               lambda bi, h, n: (bi * ng + n, (COL_GATE + 8) // 2 + 2 * h + 1)),
                  pl.BlockSpec((1, HEAD_DIM), lambda bi, h, n: (0, 0))],
        out_specs=pl.BlockSpec(blk, tok),
        out_shape=jax.ShapeDtypeStruct((m, GROUP_WIDTH), BF16),
        scratch_shapes=[pltpu.VMEM((hb, HEAD_DIM, HEAD_DIM), F32)],
        compiler_params=_params("arbitrary", "arbitrary", "arbitrary"),
        name="gated_delta_net",
    )(qn, kn, vv, gcol, bcol, grow, h2d, h2d, norm_w)


def _out_proj_kernel(oa_ref, ob_ref, oc_ref, od_ref, w_ref, x_ref, g_ref, b_ref, y_ref, y16_ref):
    o = jnp.concatenate([oa_ref[...], ob_ref[...], oc_ref[...], od_ref[...]], axis=1)
    z = _dot(o, w_ref[...]) + DEEPNORM_ALPHA * x_ref[...]
    zc = z - jnp.mean(z, axis=-1, keepdims=True)
    y = zc * lax.rsqrt(jnp.mean(zc * zc, axis=-1, keepdims=True) + 1e-5) * g_ref[...] + b_ref[...]
    y_ref[...] = y
    y16_ref[...] = y.astype(BF16)


def _out_proj_ln(oa, ob, oc, od, w16, x2d, gain, bias):
    m = x2d.shape[0]
    tm = 128
    grp = pl.BlockSpec((tm, GROUP_WIDTH), lambda i: (i, 0))
    row = pl.BlockSpec((tm, D_MODEL), lambda i: (i, 0))
    vec = pl.BlockSpec((1, D_MODEL), lambda i: (0, 0))
    weight = pl.BlockSpec((D_MODEL, D_MODEL), lambda i: (0, 0), pipeline_mode=pl.Buffered(1))
    return pl.pallas_call(
        _out_proj_kernel,
        grid=(m // tm,),
        in_specs=[grp, grp, grp, grp, weight, row, vec, vec],
        out_specs=[row, row],
        out_shape=[jax.ShapeDtypeStruct((m, D_MODEL), F32), jax.ShapeDtypeStruct((m, D_MODEL), BF16)],
        compiler_params=_params("arbitrary"),
        name="out_proj_layernorm",
    )(oa, ob, oc, od, w16, x2d, gain, bias)


def _rotate_half_cols(w):
    half = MLA_ROPE // 2
    return jnp.concatenate([-w[:, half:], w[:, :half]], axis=1)


def _layout_w_in(w):
    (a_q, a_k, a_v, b_qkv, b_a, b_b, c_dq, c_dkv, d_q, d_k, d_v, gate) = jnp.split(w, _split_points(), axis=1)
    ab = jnp.concatenate([b_a, b_b, jnp.zeros((w.shape[0], LANE - 2 * N_HEADS), w.dtype)], axis=1)
    k_rot = _rotate_half_cols(c_dkv[:, MLA_KV_RANK:])
    out = jnp.concatenate([b_qkv, c_dq, c_dkv, k_rot, ab, a_q, a_k, a_v, d_q, d_k, d_v, gate], axis=1)
    return out.astype(BF16)


def _split_points():
    pts, s = [], 0
    for n in IN_SIZES[:-1]:
        s += n
        pts.append(s)
    return pts


def _layout_w_uq(w):
    w = w.reshape(MLA_Q_RANK, N_HEADS, MLA_NOPE + MLA_ROPE)
    zeros = jnp.zeros((MLA_Q_RANK, N_HEADS, LANE - MLA_ROPE), w.dtype)
    rope = w[:, :, MLA_NOPE:]
    rot = jnp.concatenate([-rope[:, :, MLA_ROPE // 2:], rope[:, :, :MLA_ROPE // 2]], axis=2)
    w1 = jnp.concatenate([w, zeros], axis=2).reshape(MLA_Q_RANK, N_HEADS * 2 * LANE)
    w2 = jnp.concatenate([rot, zeros], axis=2).reshape(MLA_Q_RANK, N_HEADS * LANE)
    return w1.astype(BF16), w2.astype(BF16)


def _layout_w_ukv(w):
    w = w.reshape(MLA_KV_RANK, N_HEADS, MLA_NOPE + HEAD_DIM)
    wk = w[:, :, :MLA_NOPE].reshape(MLA_KV_RANK, GROUP_WIDTH)
    wv = w[:, :, MLA_NOPE:].reshape(MLA_KV_RANK, GROUP_WIDTH)
    return wk.astype(BF16), wv.astype(BF16)


def _rope_tables(t):
    half = MLA_ROPE // 2
    inv = ROPE_THETA ** (-jnp.arange(half, dtype=F32) / half)
    ang = jnp.arange(t).astype(F32)[:, None] * inv[None, :]
    zeros = jnp.zeros((t, LANE - MLA_ROPE), F32)
    cos_t = jnp.concatenate([jnp.cos(ang), jnp.cos(ang), zeros], axis=1)
    sin_t = jnp.concatenate([jnp.sin(ang), jnp.sin(ang), zeros], axis=1)
    return cos_t, sin_t


def _pad_lanes(v):
    return jnp.concatenate([v.astype(F32), jnp.zeros((LANE - v.shape[0],), F32)])[None, :]


def _layer(x2d, x16, layer_idx, b, t, w_in, diff_lambda, diff_norm, gdn_conv, gdn_a_log, gdn_dt_bias, gdn_norm,
           mla_q_norm, mla_w_uq, mla_kv_norm, mla_w_ukv, rel_bias, w_out, ln_gain, ln_bias, tables):
    h2d = _in_proj(x16, _layout_w_in(w_in))

    lam_init = 0.8 - 0.6 * math.exp(-0.3 * layer_idx)
    slopes = 2.0 ** (-8.0 * jnp.arange(1, N_HEADS + 1, dtype=F32) / N_HEADS)
    slopes = jnp.broadcast_to(slopes[:, None, None], (N_HEADS, 1, LANE))
    lam_p = jnp.concatenate([diff_lambda.astype(F32), jnp.zeros((4, LANE - DIFF_HALF), F32)], axis=1)
    o_a = _diff_attention(h2d, slopes, lam_p, diff_norm.astype(F32)[None, :], lam_init, b, t)

    qn, kn, vv, gcol, bcol, grow = _gdn_prep(h2d, gdn_conv.astype(F32), _pad_lanes(gdn_a_log),
                                              _pad_lanes(gdn_dt_bias), b, t)
    o_b = _gated_delta_net(qn, kn, vv, gcol, bcol, grow, h2d, gdn_norm.astype(F32)[None, :], b, t)

    w1, w2 = _layout_w_uq(mla_w_uq)
    wk, wv = _layout_w_ukv(mla_w_ukv)
    cos_t, sin_t = tables
    qf, kf, vc = _mla_prep(h2d, mla_q_norm.astype(F32)[None, :], mla_kv_norm.astype(F32)[None, :],
                           w1, w2, wk, wv, cos_t, sin_t, b, t)
    o_c = _mla_attention(qf, kf, vc, h2d, b, t)

    o_d = _band_attention(h2d, _band_bias_tiles(rel_bias), b, t)

    return _out_proj_ln(o_a, o_b, o_c, o_d, w_out.astype(BF16), x2d,
                        ln_gain.astype(F32)[None, :], ln_bias.astype(F32)[None, :])


def kernel(x, w_in, diff_lambda, diff_norm, gdn_conv, gdn_a_log, gdn_dt_bias, gdn_norm, mla_q_norm, mla_w_uq,
           mla_kv_norm, mla_w_ukv, rel_bias, w_out, ln_gain, ln_bias):
    b, t, d = x.shape
    assert d == D_MODEL and t % GDN_GROUP == 0 and (b * t) % 512 == 0
    tables = _rope_tables(t)
    x2d = x.reshape(b * t, d)
    x16 = x2d.astype(BF16)
    for l in range(DEPTH):
        x2d, x16 = _layer(x2d, x16, l, b, t, w_in[l], diff_lambda[l], diff_norm[l], gdn_conv[l], gdn_a_log[l],
                          gdn_dt_bias[l], gdn_norm[l], mla_q_norm[l], mla_w_uq[l], mla_kv_norm[l],
                          mla_w_ukv[l], rel_bias[l], w_out[l], ln_gain[l], ln_bias[l], tables)
    return x2d.reshape(b, t, d)
```

```python
import functools
import math

import jax
import jax.numpy as jnp
from jax import lax
from jax.experimental import pallas as pl
from jax.experimental.pallas import tpu as pltpu

F32 = jnp.float32
BF16 = jnp.bfloat16

D_MODEL = 4096
DEPTH = 2
CHUNK = 64
N_HEADS = 8
HEAD_DIM = 128
GROUP_WIDTH = N_HEADS * HEAD_DIM
DIFF_HALF = HEAD_DIM // 2
GDN_CONV = 4
MLA_Q_RANK = 768
MLA_KV_RANK = 256
MLA_NOPE = 128
MLA_ROPE = 64
ROPE_THETA = 10000.0
BAND_CHUNKS = 9
REL_CLIP = 128
DEEPNORM_ALPHA = (2 * DEPTH) ** 0.25
IN_SIZES = (GROUP_WIDTH, GROUP_WIDTH, GROUP_WIDTH, 3 * GROUP_WIDTH, N_HEADS, N_HEADS,
            MLA_Q_RANK, MLA_KV_RANK + MLA_ROPE, GROUP_WIDTH, GROUP_WIDTH, GROUP_WIDTH, D_MODEL)

LANE = 128
IN_COLS = sum(IN_SIZES)
(OFF_DIFF_Q, OFF_DIFF_K, OFF_DIFF_V, OFF_GDN_QKV, OFF_GDN_A, OFF_GDN_B, OFF_MLA_Q, OFF_MLA_KV,
 OFF_BAND_Q, OFF_BAND_K, OFF_BAND_V, OFF_GATE) = (sum(IN_SIZES[:n]) for n in range(len(IN_SIZES)))

FRONT_TILE = 768
FRONT_COLS = 10 * FRONT_TILE
BACK_TILE = 512
BACK_SHIFT = OFF_BAND_Q % LANE
BACK_COLS = IN_COLS - OFF_BAND_Q
MLA_LANE_OFF = OFF_MLA_Q % LANE
assert OFF_GDN_QKV % LANE == 0 and OFF_GDN_A % LANE == 0 and OFF_GDN_B == OFF_GDN_A + N_HEADS
assert OFF_MLA_KV % LANE == MLA_LANE_OFF and OFF_GATE % LANE == BACK_SHIFT and BACK_COLS % BACK_TILE == 0
assert FRONT_COLS >= OFF_BAND_Q and (OFF_BAND_Q - BACK_SHIFT) % BACK_TILE == 0
COL_DIFF_Q = OFF_DIFF_Q // LANE
COL_DIFF_K = OFF_DIFF_K // LANE
COL_DIFF_V = OFF_DIFF_V // LANE
COL_GDN_QKV = OFF_GDN_QKV // LANE
COL_GDN_AB = OFF_GDN_A // LANE
COL_MLA = OFF_MLA_Q // LANE
COL_BAND_Q = 0
COL_BAND_K = (OFF_BAND_K - OFF_BAND_Q) // LANE
COL_BAND_V = (OFF_BAND_V - OFF_BAND_Q) // LANE
COL_GATE = (OFF_GATE - OFF_BAND_Q) // LANE
MLA_Q_BLOCKS = MLA_Q_RANK // LANE + 1
MLA_KV_BLOCKS = 3

ATT_BLOCK = 256
GDN_GROUP = 256
NEG_BIG = -1e30
VMEM_LIMIT = 48 * 1024 * 1024


def _params(*sem):
    return pltpu.CompilerParams(dimension_semantics=sem, vmem_limit_bytes=VMEM_LIMIT)


def _dot(a, b):
    return jnp.dot(a, b, preferred_element_type=F32)


def _dot_nt(a, b):
    return lax.dot_general(a, b, (((1,), (1,)), ((), ())), preferred_element_type=F32)


def _silu(x):
    return x * (1.0 / (1.0 + jnp.exp(-x)))


CAST_ROWS = 256


def _in_proj_front_kernel(x_ref, w_ref, o_ref, w16_ref):
    @pl.when(pl.program_id(1) == 0)
    def _():
        def body(r, carry):
            rows = pl.ds(pl.multiple_of(r * CAST_ROWS, CAST_ROWS), CAST_ROWS)
            w16_ref[rows, :] = w_ref[rows, :].astype(BF16)
            return carry
        lax.fori_loop(0, D_MODEL // CAST_ROWS, body, 0)

    o_ref[...] = _dot(x_ref[...], w16_ref[...]).astype(o_ref.dtype)


def _in_proj_back_kernel(x_ref, w_ref, w_next_ref, o_ref, w16_ref):
    @pl.when(pl.program_id(1) == 0)
    def _():
        n_blocks = w_ref.shape[1] // LANE
        take = LANE - BACK_SHIFT
        from_same = lax.broadcasted_iota(jnp.int32, (CAST_ROWS, LANE), 1) < take

        def body(r, carry):
            rows = pl.ds(pl.multiple_of(r * CAST_ROWS, CAST_ROWS), CAST_ROWS)
            cur = pltpu.roll(w_ref[rows, 0:LANE], take, axis=1)
            for blk in range(n_blocks):
                src = w_ref[rows, LANE * (blk + 1):LANE * (blk + 2)] if blk + 1 < n_blocks else w_next_ref[rows, :]
                nxt = pltpu.roll(src, take, axis=1)
                w16_ref[rows, LANE * blk:LANE * (blk + 1)] = jnp.where(from_same, cur, nxt).astype(BF16)
                cur = nxt
            return carry
        lax.fori_loop(0, D_MODEL // CAST_ROWS, body, 0)

    o_ref[...] = _dot(x_ref[...], w16_ref[...]).astype(o_ref.dtype)


def _in_proj(xb, w_in, layer):
    m = xb.shape[0]
    tm = 512
    x_spec = pl.BlockSpec((tm, D_MODEL), lambda j, i: (i, 0))
    front = pl.pallas_call(
        _in_proj_front_kernel,
        grid=(FRONT_COLS // FRONT_TILE, m // tm),
        in_specs=[x_spec, pl.BlockSpec((None, D_MODEL, FRONT_TILE), lambda j, i: (layer, 0, j))],
        out_specs=pl.BlockSpec((tm, FRONT_TILE), lambda j, i: (i, j)),
        out_shape=jax.ShapeDtypeStruct((m, FRONT_COLS), BF16),
        scratch_shapes=[pltpu.VMEM((D_MODEL, FRONT_TILE), BF16)],
        compiler_params=_params("arbitrary", "arbitrary"),
        name="in_proj_front",
    )(xb, w_in)
    first_tile = (OFF_BAND_Q - BACK_SHIFT) // BACK_TILE
    tile_blocks = BACK_TILE // LANE
    back = pl.pallas_call(
        _in_proj_back_kernel,
        grid=(BACK_COLS // BACK_TILE, m // tm),
        in_specs=[x_spec,
                  pl.BlockSpec((None, D_MODEL, BACK_TILE), lambda j, i: (layer, 0, first_tile + j)),
                  pl.BlockSpec((None, D_MODEL, LANE), lambda j, i: (layer, 0, (first_tile + j + 1) * tile_blocks))],
        out_specs=pl.BlockSpec((tm, BACK_TILE), lambda j, i: (i, j)),
        out_shape=jax.ShapeDtypeStruct((m, BACK_COLS), BF16),
        scratch_shapes=[pltpu.VMEM((D_MODEL, BACK_TILE), BF16)],
        compiler_params=_params("arbitrary", "arbitrary"),
        name="in_proj_back",
    )(xb, w_in, w_in)
    return front, back


def _softmax_stats(m, l, t, shift):
    m_new = jnp.maximum(m, jnp.max(t, axis=-1, keepdims=True) + shift)
    alpha = jnp.exp(m - m_new)
    p = jnp.exp(t - (m_new - shift))
    return m_new, alpha * l + jnp.sum(p, axis=-1, keepdims=True), alpha, p.astype(BF16)


def _attn_sweep(qs, k_ref, v_ref, lo, i, off_scores, diag_scores):
    tq = qs[0].shape[0]

    def rows(ref, kb):
        return ref[pl.ds(pl.multiple_of(kb * ATT_BLOCK, ATT_BLOCK), ATT_BLOCK), :]

    def body(kb, carry):
        k_next = rows(k_ref, kb + 1)
        v_prev = rows(v_ref, jnp.maximum(kb - 1, lo))
        new = []
        for (m, l, acc, s, p_prev, a_prev), q in zip(carry, qs):
            s_next = _dot_nt(q, k_next)
            acc = a_prev * acc + _dot(p_prev, v_prev)
            m, l, alpha, p = _softmax_stats(m, l, *off_scores(s, kb))
            new.append((m, l, acc, s_next, p, alpha))
        return tuple(new)

    k_lo = rows(k_ref, lo)
    init = tuple((jnp.full((tq, 1), NEG_BIG, F32), jnp.zeros((tq, 1), F32), jnp.zeros((tq, HEAD_DIM), F32),
                  _dot_nt(q, k_lo), jnp.zeros((tq, ATT_BLOCK), BF16), jnp.ones((tq, 1), F32)) for q in qs)
    carry = lax.fori_loop(lo, i, body, init)
    v_prev = rows(v_ref, jnp.maximum(i - 1, lo))
    v_diag = rows(v_ref, i)
    out = []
    for m, l, acc, s, p_prev, a_prev in carry:
        acc = a_prev * acc + _dot(p_prev, v_prev)
        m, l, alpha, p = _softmax_stats(m, l, diag_scores(s), 0.0)
        out.append((alpha * acc + _dot(p, v_diag), l))
    return out


def _block_iotas():
    r = lax.broadcasted_iota(jnp.int32, (ATT_BLOCK, ATT_BLOCK), 0)
    c = lax.broadcasted_iota(jnp.int32, (ATT_BLOCK, ATT_BLOCK), 1)
    return r, c


def _diff_attn_kernel(lam_init, q_ref, k_ref, v_ref, gate_ref, slope_ref, lam_ref, nw_ref, o_ref):
    i = pl.program_id(2)
    q = q_ref[...] * (DIFF_HALF ** -0.5)
    lane = lax.broadcasted_iota(jnp.int32, q.shape, 1)
    q1 = jnp.where(lane < DIFF_HALF, q, jnp.zeros_like(q))
    q2 = jnp.where(lane >= DIFF_HALF, q, jnp.zeros_like(q))
    r, c = _block_iotas()
    rc = (r - c).astype(F32)
    slope = slope_ref[:, 0:1]
    bias_off = -slope * rc
    bias_diag = jnp.where((c // CHUNK) <= (r // CHUNK), -slope * jnp.abs(rc), NEG_BIG)

    def off_scores(s, kb):
        return s + bias_off, -slope * ((i - kb) * ATT_BLOCK).astype(F32)

    def diag_scores(s):
        return s + bias_diag

    (acc1, l1), (acc2, l2) = _attn_sweep((q1, q2), k_ref, v_ref, 0, i, off_scores, diag_scores)
    lp = lam_ref[...]
    lam = (jnp.exp(jnp.sum(lp[0:1] * lp[1:2], axis=-1, keepdims=True))
           - jnp.exp(jnp.sum(lp[2:3] * lp[3:4], axis=-1, keepdims=True)) + lam_init)
    o = acc1 / l1 - lam * (acc2 / l2)
    o = o * lax.rsqrt(jnp.mean(o * o, axis=-1, keepdims=True) + 1e-6) * nw_ref[...]
    o = o * (1.0 - lam_init)
    o_ref[...] = (o * _silu(gate_ref[...].astype(F32))).astype(o_ref.dtype)


def _diff_attention(front, back, slopes, lam_p, norm_w, lam_init, b, t):
    nq = t // ATT_BLOCK
    m = b * t
    blk = (ATT_BLOCK, HEAD_DIM)
    return pl.pallas_call(
        functools.partial(_diff_attn_kernel, lam_init),
        grid=(b, N_HEADS, nq),
        in_specs=[pl.BlockSpec(blk, lambda bi, h, i: (bi * nq + i, COL_DIFF_Q + h)),
                  pl.BlockSpec((t, HEAD_DIM), lambda bi, h, i: (bi, COL_DIFF_K + h)),
                  pl.BlockSpec((t, HEAD_DIM), lambda bi, h, i: (bi, COL_DIFF_V + h)),
                  pl.BlockSpec(blk, lambda bi, h, i: (bi * nq + i, COL_GATE + h)),
                  pl.BlockSpec((None, 1, LANE), lambda bi, h, i: (h, 0, 0)),
                  pl.BlockSpec((4, LANE), lambda bi, h, i: (0, 0)),
                  pl.BlockSpec((1, HEAD_DIM), lambda bi, h, i: (0, 0))],
        out_specs=pl.BlockSpec(blk, lambda bi, h, i: (bi * nq + i, h)),
        out_shape=jax.ShapeDtypeStruct((m, GROUP_WIDTH), BF16),
        compiler_params=_params("arbitrary", "arbitrary", "arbitrary"),
        name="diff_attention",
    )(front, front, front, back, slopes, lam_p, norm_w)


def _mla_attn_kernel(q_ref, k_ref, v_ref, gate_ref, o_ref):
    i = pl.program_id(2)
    r, c = _block_iotas()
    mask_diag = jnp.where((c // CHUNK) <= (r // CHUNK), 0.0, NEG_BIG)
    scale = (MLA_NOPE + MLA_ROPE) ** -0.5

    def off_scores(s, kb):
        return s * scale, 0.0

    def diag_scores(s):
        return s * scale + mask_diag

    ((acc, l),) = _attn_sweep((q_ref[...],), k_ref, v_ref, 0, i, off_scores, diag_scores)
    o_ref[...] = ((acc / l) * _silu(gate_ref[...].astype(F32))).astype(o_ref.dtype)


def _mla_attention(qf, kf, vv, back, b, t):
    nq = t // ATT_BLOCK
    m = b * t
    return pl.pallas_call(
        _mla_attn_kernel,
        grid=(b, N_HEADS, nq),
        in_specs=[pl.BlockSpec((ATT_BLOCK, 2 * LANE), lambda bi, h, i: (bi * nq + i, h)),
                  pl.BlockSpec((t, 2 * LANE), lambda bi, h, i: (bi, h)),
                  pl.BlockSpec((t, HEAD_DIM), lambda bi, h, i: (bi, h)),
                  pl.BlockSpec((ATT_BLOCK, HEAD_DIM), lambda bi, h, i: (bi * nq + i, COL_GATE + 16 + h))],
        out_specs=pl.BlockSpec((ATT_BLOCK, HEAD_DIM), lambda bi, h, i: (bi * nq + i, h)),
        out_shape=jax.ShapeDtypeStruct((m, GROUP_WIDTH), BF16),
        compiler_params=_params("arbitrary", "arbitrary", "arbitrary"),
        name="mla_attention",
    )(qf, kf, vv, back)


BAND_KEY_BLOCKS = (BAND_CHUNKS - 1) * CHUNK // ATT_BLOCK + 1


def _band_attn_kernel(q_ref, k_ref, v_ref, gate_ref, bias_ref, o_ref):
    i = pl.program_id(2)
    scale = HEAD_DIM ** -0.5

    def off_scores(s, kb):
        return s * scale + bias_ref[kb - i + (BAND_KEY_BLOCKS - 1)], 0.0

    def diag_scores(s):
        return s * scale + bias_ref[BAND_KEY_BLOCKS - 1]

    lo = jnp.maximum(i - (BAND_KEY_BLOCKS - 1), 0)
    ((acc, l),) = _attn_sweep((q_ref[...],), k_ref, v_ref, lo, i, off_scores, diag_scores)
    o_ref[...] = ((acc / l) * _silu(gate_ref[...].astype(F32))).astype(o_ref.dtype)


def _band_attention(back, bias_tiles, b, t):
    nq = t // ATT_BLOCK
    m = b * t
    blk = (ATT_BLOCK, HEAD_DIM)
    return pl.pallas_call(
        _band_attn_kernel,
        grid=(b, N_HEADS, nq),
        in_specs=[pl.BlockSpec(blk, lambda bi, h, i: (bi * nq + i, COL_BAND_Q + h)),
                  pl.BlockSpec((t, HEAD_DIM), lambda bi, h, i: (bi, COL_BAND_K + h)),
                  pl.BlockSpec((t, HEAD_DIM), lambda bi, h, i: (bi, COL_BAND_V + h)),
                  pl.BlockSpec(blk, lambda bi, h, i: (bi * nq + i, COL_GATE + 24 + h)),
                  pl.BlockSpec((None, BAND_KEY_BLOCKS, ATT_BLOCK, ATT_BLOCK), lambda bi, h, i: (h, 0, 0, 0))],
        out_specs=pl.BlockSpec(blk, lambda bi, h, i: (bi * nq + i, h)),
        out_shape=jax.ShapeDtypeStruct((m, GROUP_WIDTH), BF16),
        compiler_params=_params("arbitrary", "arbitrary", "arbitrary"),
        name="band_attention",
    )(back, back, back, back, bias_tiles)


def _band_bias_tiles(rel_bias):
    n = ATT_BLOCK
    back = (BAND_KEY_BLOCKS - 1) - jnp.arange(BAND_KEY_BLOCKS)
    j = jnp.arange(2 * n)
    c_minus_r = jnp.where(j < n, j, j - 2 * n)
    rel = back[:, None] * n - c_minus_r[None, :]
    vals = rel_bias.astype(F32)[:, jnp.clip(rel, -REL_CLIP, REL_CLIP) + REL_CLIP]
    flat = jnp.tile(vals, (1, 1, n))[:, :, :n * (2 * n - 1)]
    tiles = flat.reshape(N_HEADS, BAND_KEY_BLOCKS, n, 2 * n - 1)[:, :, :, :n]
    r = jnp.arange(n)[None, :, None]
    c = jnp.arange(n)[None, None, :]
    chunk_back = back[:, None, None] * (n // CHUNK) + r // CHUNK - c // CHUNK
    visible = (chunk_back >= 0) & (chunk_back < BAND_CHUNKS)
    return jnp.where(visible[None], tiles, NEG_BIG)


def _mla_prep_kernel(cdq_ref, ckv_ref, qg_ref, kvg_ref, w1_ref, w2_ref, wk_ref, wv_ref, ct_ref, st_ref,
                     q_out, k_out, v_out):
    ct = ct_ref[...]
    st = st_ref[...]
    ckv = ckv_ref[...].astype(F32)
    cq = jnp.concatenate([cdq_ref[...].astype(F32), ckv[:, :LANE]], axis=1)
    lane_q = lax.broadcasted_iota(jnp.int32, (1, cq.shape[1]), 1)
    cq = jnp.where((lane_q >= MLA_LANE_OFF) & (lane_q < MLA_LANE_OFF + MLA_Q_RANK), cq, 0.0)
    ms_q = jnp.sum(cq * cq, axis=-1, keepdims=True) * (1.0 / MLA_Q_RANK)
    nq = (cq * lax.rsqrt(ms_q + 1e-6) * qg_ref[...]).astype(BF16)
    q1 = _dot(nq, w1_ref[...])
    q2 = _dot(nq, w2_ref[...])
    lane_kv = lax.broadcasted_iota(jnp.int32, (1, ckv.shape[1]), 1)
    lat = jnp.where((lane_kv >= MLA_LANE_OFF) & (lane_kv < MLA_LANE_OFF + MLA_KV_RANK), ckv, 0.0)
    ms_kv = jnp.sum(lat * lat, axis=-1, keepdims=True) * (1.0 / MLA_KV_RANK)
    nkv = (lat * lax.rsqrt(ms_kv + 1e-6) * kvg_ref[...]).astype(BF16)
    kn = _dot(nkv, wk_ref[...])
    v_out[...] = _dot(nkv, wv_ref[...]).astype(v_out.dtype)
    kr = ckv[:, 2 * LANE:]
    half = MLA_ROPE // 2
    lane_r = lax.broadcasted_iota(jnp.int32, (1, LANE), 1)
    rot = jnp.where(lane_r < MLA_LANE_OFF + half, -pltpu.roll(kr, LANE - half, axis=1), pltpu.roll(kr, half, axis=1))
    kr = (kr * ct + rot * st).astype(k_out.dtype)
    for h in range(N_HEADS):
        a, bq = 2 * LANE * h, LANE * h
        q_out[:, a:a + LANE] = q1[:, a:a + LANE].astype(q_out.dtype)
        q_out[:, a + LANE:a + 2 * LANE] = (q1[:, a + LANE:a + 2 * LANE] * ct
                                           + q2[:, bq:bq + LANE] * st).astype(q_out.dtype)
        k_out[:, a:a + LANE] = kn[:, bq:bq + LANE].astype(k_out.dtype)
        k_out[:, a + LANE:a + 2 * LANE] = kr


def _mla_prep(front, q_gain, kv_gain, w1, w2, wk, wv, cos_t, sin_t, b, t):
    m = b * t
    tm = 256
    nt = t // tm
    full = lambda shape: pl.BlockSpec(shape, lambda i: (0, 0))
    q_blocks = MLA_Q_BLOCKS - 1
    assert COL_MLA % q_blocks == 0 and (COL_MLA + q_blocks) % MLA_KV_BLOCKS == 0
    return pl.pallas_call(
        _mla_prep_kernel,
        grid=(m // tm,),
        in_specs=[pl.BlockSpec((tm, q_blocks * LANE), lambda i: (i, COL_MLA // q_blocks)),
                  pl.BlockSpec((tm, MLA_KV_BLOCKS * LANE), lambda i: (i, (COL_MLA + q_blocks) // MLA_KV_BLOCKS)),
                  full(q_gain.shape), full(kv_gain.shape),
                  full(w1.shape), full(w2.shape), full(wk.shape), full(wv.shape),
                  pl.BlockSpec((tm, LANE), lambda i: (i % nt, 0)),
                  pl.BlockSpec((tm, LANE), lambda i: (i % nt, 0))],
        out_specs=[pl.BlockSpec((tm, 2 * GROUP_WIDTH), lambda i: (i, 0)),
                   pl.BlockSpec((tm, 2 * GROUP_WIDTH), lambda i: (i, 0)),
                   pl.BlockSpec((tm, GROUP_WIDTH), lambda i: (i, 0))],
        out_shape=[jax.ShapeDtypeStruct((m, 2 * GROUP_WIDTH), BF16),
                   jax.ShapeDtypeStruct((m, 2 * GROUP_WIDTH), BF16),
                   jax.ShapeDtypeStruct((m, GROUP_WIDTH), BF16)],
        compiler_params=_params("arbitrary"),
        name="mla_prep",
    )(front, front, q_gain, kv_gain, w1, w2, wk, wv, cos_t, sin_t)


HALO = 16


def _gdn_prep_kernel(blocks_per_seq, x_ref, halo_ref, ab_ref, cw_ref, alog_ref, dtb_ref,
                     qn_ref, kn_ref, v_ref, gcol_ref, bcol_ref, grow_ref):
    i = pl.program_id(0)
    halo_scale = jnp.where(i % blocks_per_seq == 0, 0.0, 1.0)
    outs = (qn_ref, kn_ref, v_ref)
    for j in range(3 * N_HEADS):
        sl = slice(LANE * j, LANE * (j + 1))
        cat = jnp.concatenate([halo_ref[:, sl].astype(F32) * halo_scale, x_ref[:, sl].astype(F32)], axis=0)
        w = cw_ref[:, sl]
        acc = cat * w[GDN_CONV - 1:GDN_CONV]
        for s in range(1, GDN_CONV):
            acc = acc + pltpu.roll(cat, s, axis=0) * w[GDN_CONV - 1 - s:GDN_CONV - s]
        y = _silu(acc[HALO:])
        part, head = divmod(j, N_HEADS)
        if part < 2:
            y = y * lax.rsqrt(jnp.sum(y * y, axis=-1, keepdims=True) + 1e-6)
        if part == 0:
            y = y * HEAD_DIM ** -0.5
        outs[part][:, LANE * head:LANE * (head + 1)] = y.astype(BF16)

    ab = ab_ref[...].astype(F32)
    z = ab + dtb_ref[...]
    softplus = jnp.maximum(z, 0.0) + jnp.log1p(jnp.exp(-jnp.abs(z)))
    g = -jnp.exp(alog_ref[...]) * softplus
    beta = 1.0 / (1.0 + jnp.exp(-ab))
    row = lax.broadcasted_iota(jnp.int32, g.shape, 0) % CHUNK
    s = 1
    while s < CHUNK:
        g = g + jnp.where(row >= s, pltpu.roll(g, s, axis=0), 0.0)
        s *= 2
    gt = g.T
    for h in range(N_HEADS):
        gcol_ref[h] = jnp.broadcast_to(g[:, h:h + 1], g.shape)
        bcol_ref[h] = jnp.broadcast_to(beta[:, N_HEADS + h:N_HEADS + h + 1], g.shape)
        grow_ref[h] = gt[h:h + 1, :]


def _gdn_prep(front, conv_w, alog, dtb, b, t):
    m = b * t
    tm = GDN_GROUP
    nt = t // tm
    assert OFF_GDN_QKV % (3 * GROUP_WIDTH) == 0
    qkv_block = OFF_GDN_QKV // (3 * GROUP_WIDTH)
    return pl.pallas_call(
        functools.partial(_gdn_prep_kernel, nt),
        grid=(m // tm,),
        in_specs=[pl.BlockSpec((tm, 3 * GROUP_WIDTH), lambda i: (i, qkv_block)),
                  pl.BlockSpec((HALO, 3 * GROUP_WIDTH), lambda i: (jnp.maximum(i * (tm // HALO) - 1, 0), qkv_block)),
                  pl.BlockSpec((tm, LANE), lambda i: (i, COL_GDN_AB)),
                  pl.BlockSpec((GDN_CONV, 3 * GROUP_WIDTH), lambda i: (0, 0)),
                  pl.BlockSpec((1, LANE), lambda i: (0, 0)),
                  pl.BlockSpec((1, LANE), lambda i: (0, 0))],
        out_specs=[pl.BlockSpec((tm, GROUP_WIDTH), lambda i: (i, 0)),
                   pl.BlockSpec((tm, GROUP_WIDTH), lambda i: (i, 0)),
                   pl.BlockSpec((tm, GROUP_WIDTH), lambda i: (i, 0)),
                   pl.BlockSpec((None, N_HEADS, tm, LANE), lambda i: (i // nt, 0, i % nt, 0)),
                   pl.BlockSpec((None, N_HEADS, tm, LANE), lambda i: (i // nt, 0, i % nt, 0)),
                   pl.BlockSpec((None, N_HEADS, 1, tm), lambda i: (i // nt, 0, 0, i % nt))],
        out_shape=[jax.ShapeDtypeStruct((m, GROUP_WIDTH), BF16),
                   jax.ShapeDtypeStruct((m, GROUP_WIDTH), BF16),
                   jax.ShapeDtypeStruct((m, GROUP_WIDTH), BF16),
                   jax.ShapeDtypeStruct((b, N_HEADS, t, LANE), F32),
                   jax.ShapeDtypeStruct((b, N_HEADS, t, LANE), F32),
                   jax.ShapeDtypeStruct((b, N_HEADS, 1, t), F32)],
        compiler_params=_params("arbitrary"),
        name="gdn_prep",
    )(front, front, front, conv_w, alog, dtb)


GDN_HEADS_PER_STEP = 4


def _gdn_kernel(q_ref, k_ref, v_ref, gc_ref, bc_ref, gr_ref, gate_ref, nw_ref, o_ref, state_ref):
    n = pl.program_id(2)

    @pl.when(n == 0)
    def _():
        state_ref[...] = jnp.zeros_like(state_ref)

    g_sz = GDN_GROUP
    ri = lax.broadcasted_iota(jnp.int32, (g_sz, g_sz), 0)
    ci = lax.broadcasted_iota(jnp.int32, (g_sz, g_sz), 1)
    dif = jnp.where((ri // CHUNK) == (ci // CHUNK), ri - ci, -1)
    heads = range(GDN_HEADS_PER_STEP)
    cols = [slice(HEAD_DIM * hb, HEAD_DIM * (hb + 1)) for hb in heads]
    outs = _gdn_heads(dif, [q_ref[:, sl] for sl in cols], [k_ref[:, sl] for sl in cols],
                      [v_ref[:, sl] for sl in cols], [gc_ref[hb] for hb in heads], [bc_ref[hb] for hb in heads],
                      [gr_ref[hb] for hb in heads], state_ref)
    for hb in heads:
        o = outs[hb]
        o = o * lax.rsqrt(jnp.mean(o * o, axis=-1, keepdims=True) + 1e-6) * nw_ref[...]
        o_ref[:, cols[hb]] = (o * _silu(gate_ref[:, cols[hb]].astype(F32))).astype(o_ref.dtype)


def _gdn_heads(dif, q16, k16, v16, gc, beta, gr, state_ref):
    heads = range(len(q16))
    n_chunks = GDN_GROUP // CHUNK
    incl = dif >= 0
    kf = [k16[h].astype(F32) for h in heads]
    eg = [jnp.exp(gc[h]) for h in heads]
    kbeta = [kf[h] * beta[h] for h in heads]
    decay = [jnp.where(incl, jnp.exp(jnp.where(incl, jnp.concatenate([gc[h], gc[h]], axis=1) - gr[h], 0.0)), 0.0)
             for h in heads]

    kk = [_dot_nt(kbeta[h].astype(BF16), k16[h]) for h in heads]
    mpow = [jnp.where(dif > 0, -(kk[h] * decay[h]), 0.0) for h in heads]
    inv = [jnp.where(dif == 0, 1.0, 0.0) + mpow[h] for h in heads]
    mpow16 = [mpow[h].astype(BF16) for h in heads]
    for _ in range(int(math.log2(CHUNK)) - 1):
        mpow16 = [_dot(mpow16[h], mpow16[h]).astype(BF16) for h in heads]
        inv = [inv[h] + _dot(inv[h].astype(BF16), mpow16[h]) for h in heads]

    rhs = [jnp.concatenate([v16[h].astype(F32) * beta[h], kbeta[h] * eg[h]], axis=1).astype(BF16) for h in heads]
    uw = [_dot(inv[h].astype(BF16), rhs[h]) for h in heads]
    u = [uw[h][:, :HEAD_DIM] for h in heads]
    w = [uw[h][:, HEAD_DIM:].astype(BF16) for h in heads]
    qk = [(_dot_nt(q16[h], k16[h]) * decay[h]).astype(BF16) for h in heads]
    qe = [(q16[h].astype(F32) * eg[h]).astype(BF16) for h in heads]
    g_last_rows = [[gc[h][CHUNK * (c + 1) - 1:CHUNK * (c + 1), :] for c in range(n_chunks)] for h in heads]
    kdec_t = [(kf[h] * jnp.exp(jnp.concatenate([jnp.broadcast_to(g, (CHUNK, LANE)) for g in g_last_rows[h]], axis=0)
                               - gc[h])).T.astype(BF16) for h in heads]
    col_chunk = lax.broadcasted_iota(jnp.int32, kdec_t[0].shape, 1) // CHUNK

    vparts = [[u[h][CHUNK * c:CHUNK * (c + 1)] for c in range(n_chunks)] for h in heads]
    state = [state_ref[h] for h in heads]
    outs = [[] for _ in heads]
    for c in range(n_chunks):
        rows = slice(CHUNK * c, CHUNK * (c + 1))
        s16 = [state[h].astype(BF16) for h in heads]
        ws = [_dot(w[h][rows], s16[h]) for h in heads]
        for h in heads:
            vparts[h][c] = u[h][rows] - ws[h]
        vfull = [jnp.concatenate(vparts[h], axis=0).astype(BF16) for h in heads]
        upd = [_dot(jnp.where(col_chunk == c, kdec_t[h], jnp.zeros_like(kdec_t[h])), vfull[h]) for h in heads]
        state = [state[h] * jnp.exp(g_last_rows[h][c]) + upd[h] for h in heads]
        for h in heads:
            outs[h].append(_dot(qe[h][rows], s16[h]) + _dot(qk[h][rows], vfull[h]))
    for h in heads:
        state_ref[h] = state[h]
    return [jnp.concatenate(outs[h], axis=0) for h in heads]


def _gated_delta_net(qn, kn, vv, gcol, bcol, grow, back, norm_w, b, t):
    assert (COL_GATE + N_HEADS) % GDN_HEADS_PER_STEP == 0
    m = b * t
    ng = t // GDN_GROUP
    hb = GDN_HEADS_PER_STEP
    blk = (GDN_GROUP, HEAD_DIM * hb)
    tok = lambda bi, h, n: (bi * ng + n, h)
    return pl.pallas_call(
        _gdn_kernel,
        grid=(b, N_HEADS // hb, ng),
        in_specs=[pl.BlockSpec(blk, tok), pl.BlockSpec(blk, tok), pl.BlockSpec(blk, tok),
                  pl.BlockSpec((None, hb, GDN_GROUP, LANE), lambda bi, h, n: (bi, h, n, 0)),
                  pl.BlockSpec((None, hb, GDN_GROUP, LANE), lambda bi, h, n: (bi, h, n, 0)),
                  pl.BlockSpec((None, hb, 1, GDN_GROUP), lambda bi, h, n: (bi, h, 0, n)),
                  pl.BlockSpec(blk, lambda bi, h, n: (bi * ng + n, (COL_GATE + N_HEADS) // hb + h)),
                  pl.BlockSpec((1, HEAD_DIM), lambda bi, h, n: (0, 0))],
        out_specs=pl.BlockSpec(blk, tok),
        out_shape=jax.ShapeDtypeStruct((m, GROUP_WIDTH), BF16),
        scratch_shapes=[pltpu.VMEM((hb, HEAD_DIM, HEAD_DIM), F32)],
        compiler_params=_params("arbitrary", "arbitrary", "arbitrary"),
        name="gated_delta_net",
    )(qn, kn, vv, gcol, bcol, grow, back, norm_w)


def _out_proj_kernel(oa_ref, ob_ref, oc_ref, od_ref, w_ref, x_ref, g_ref, b_ref, y_ref, y16_ref):
    o = jnp.concatenate([oa_ref[...], ob_ref[...], oc_ref[...], od_ref[...]], axis=1)
    z = _dot(o, w_ref[...]) + DEEPNORM_ALPHA * x_ref[...]
    zc = z - jnp.mean(z, axis=-1, keepdims=True)
    y = zc * lax.rsqrt(jnp.mean(zc * zc, axis=-1, keepdims=True) + 1e-5) * g_ref[...] + b_ref[...]
    y_ref[...] = y
    y16_ref[...] = y.astype(BF16)


def _out_proj_ln(oa, ob, oc, od, w16, x2d, gain, bias):
    m = x2d.shape[0]
    tm = 128
    grp = pl.BlockSpec((tm, GROUP_WIDTH), lambda i: (i, 0))
    row = pl.BlockSpec((tm, D_MODEL), lambda i: (i, 0))
    vec = pl.BlockSpec((1, D_MODEL), lambda i: (0, 0))
    weight = pl.BlockSpec((D_MODEL, D_MODEL), lambda i: (0, 0), pipeline_mode=pl.Buffered(1))
    return pl.pallas_call(
        _out_proj_kernel,
        grid=(m // tm,),
        in_specs=[grp, grp, grp, grp, weight, row, vec, vec],
        out_specs=[row, row],
        out_shape=[jax.ShapeDtypeStruct((m, D_MODEL), F32), jax.ShapeDtypeStruct((m, D_MODEL), BF16)],
        compiler_params=_params("arbitrary"),
        name="out_proj_layernorm",
    )(oa, ob, oc, od, w16, x2d, gain, bias)


def _place(a, axis, offset, size):
    pad = [(0, 0)] * a.ndim
    pad[axis] = (offset, size - offset - a.shape[axis])
    return jnp.pad(a, pad)


def _layout_w_uq(w):
    w = w.reshape(MLA_Q_RANK, N_HEADS, MLA_NOPE + MLA_ROPE)
    rope = w[:, :, MLA_NOPE:]
    rot = jnp.concatenate([-rope[:, :, MLA_ROPE // 2:], rope[:, :, :MLA_ROPE // 2]], axis=2)
    w1 = jnp.concatenate([w[:, :, :MLA_NOPE], _place(rope, 2, MLA_LANE_OFF, LANE)], axis=2)
    w1 = w1.reshape(MLA_Q_RANK, N_HEADS * 2 * LANE)
    w2 = _place(rot, 2, MLA_LANE_OFF, LANE).reshape(MLA_Q_RANK, N_HEADS * LANE)
    rows = MLA_Q_BLOCKS * LANE
    return _place(w1, 0, MLA_LANE_OFF, rows).astype(BF16), _place(w2, 0, MLA_LANE_OFF, rows).astype(BF16)


def _layout_w_ukv(w):
    w = w.reshape(MLA_KV_RANK, N_HEADS, MLA_NOPE + HEAD_DIM)
    wk = w[:, :, :MLA_NOPE].reshape(MLA_KV_RANK, GROUP_WIDTH)
    wv = w[:, :, MLA_NOPE:].reshape(MLA_KV_RANK, GROUP_WIDTH)
    rows = MLA_KV_BLOCKS * LANE
    return _place(wk, 0, MLA_LANE_OFF, rows).astype(BF16), _place(wv, 0, MLA_LANE_OFF, rows).astype(BF16)


def _rope_tables(t):
    half = MLA_ROPE // 2
    inv = ROPE_THETA ** (-jnp.arange(half, dtype=F32) / half)
    ang = jnp.arange(t).astype(F32)[:, None] * inv[None, :]
    cos_t = _place(jnp.concatenate([jnp.cos(ang), jnp.cos(ang)], axis=1), 1, MLA_LANE_OFF, LANE)
    sin_t = _place(jnp.concatenate([jnp.sin(ang), jnp.sin(ang)], axis=1), 1, MLA_LANE_OFF, LANE)
    return cos_t, sin_t


def _pad_lanes(v):
    return jnp.concatenate([v.astype(F32), jnp.zeros((LANE - v.shape[0],), F32)])[None, :]


def _layer(x2d, x16, layer_idx, b, t, w_in, diff_lambda, diff_norm, gdn_conv, gdn_a_log, gdn_dt_bias, gdn_norm,
           mla_q_norm, mla_w_uq, mla_kv_norm, mla_w_ukv, rel_bias, w_out, ln_gain, ln_bias, tables):
    front, back = _in_proj(x16, w_in, layer_idx)

    lam_init = 0.8 - 0.6 * math.exp(-0.3 * layer_idx)
    slopes = 2.0 ** (-8.0 * jnp.arange(1, N_HEADS + 1, dtype=F32) / N_HEADS)
    slopes = jnp.broadcast_to(slopes[:, None, None], (N_HEADS, 1, LANE))
    lam_p = jnp.concatenate([diff_lambda.astype(F32), jnp.zeros((4, LANE - DIFF_HALF), F32)], axis=1)
    o_a = _diff_attention(front, back, slopes, lam_p, diff_norm.astype(F32)[None, :], lam_init, b, t)

    qn, kn, vv, gcol, bcol, grow = _gdn_prep(front, gdn_conv.astype(F32), _pad_lanes(gdn_a_log),
                                              _pad_lanes(gdn_dt_bias), b, t)
    o_b = _gated_delta_net(qn, kn, vv, gcol, bcol, grow, back, gdn_norm.astype(F32)[None, :], b, t)

    w1, w2 = _layout_w_uq(mla_w_uq)
    wk, wv = _layout_w_ukv(mla_w_ukv)
    cos_t, sin_t = tables
    q_gain = _place(mla_q_norm.astype(F32)[None, :], 1, MLA_LANE_OFF, MLA_Q_BLOCKS * LANE)
    kv_gain = _place(mla_kv_norm.astype(F32)[None, :], 1, MLA_LANE_OFF, MLA_KV_BLOCKS * LANE)
    qf, kf, vc = _mla_prep(front, q_gain, kv_gain, w1, w2, wk, wv, cos_t, sin_t, b, t)
    o_c = _mla_attention(qf, kf, vc, back, b, t)

    o_d = _band_attention(back, _band_bias_tiles(rel_bias), b, t)

    return _out_proj_ln(o_a, o_b, o_c, o_d, w_out.astype(BF16), x2d,
                        ln_gain.astype(F32)[None, :], ln_bias.astype(F32)[None, :])


def kernel(x, w_in, diff_lambda, diff_norm, gdn_conv, gdn_a_log, gdn_dt_bias, gdn_norm, mla_q_norm, mla_w_uq,
           mla_kv_norm, mla_w_ukv, rel_bias, w_out, ln_gain, ln_bias):
    b, t, d = x.shape
    assert d == D_MODEL and t % GDN_GROUP == 0 and (b * t) % 512 == 0
    tables = _rope_tables(t)
    x2d = x.reshape(b * t, d)
    x16 = x2d.astype(BF16)
    for l in range(DEPTH):
        x2d, x16 = _layer(x2d, x16, l, b, t, w_in, diff_lambda[l], diff_norm[l], gdn_conv[l], gdn_a_log[l],
                          gdn_dt_bias[l], gdn_norm[l], mla_q_norm[l], mla_w_uq[l], mla_kv_norm[l],
                          mla_w_ukv[l], rel_bias[l], w_out[l], ln_gain[l], ln_bias[l], tables)
    return x2d.reshape(b, t, d)
```

```python
import functools
import math

import jax
import jax.numpy as jnp
from jax import lax
from jax.experimental import pallas as pl
from jax.experimental.pallas import tpu as pltpu

F32 = jnp.float32
BF16 = jnp.bfloat16

D_MODEL = 4096
DEPTH = 2
CHUNK = 64
N_HEADS = 8
HEAD_DIM = 128
GROUP_WIDTH = N_HEADS * HEAD_DIM
DIFF_HALF = HEAD_DIM // 2
GDN_CONV = 4
MLA_Q_RANK = 768
MLA_KV_RANK = 256
MLA_NOPE = 128
MLA_ROPE = 64
ROPE_THETA = 10000.0
BAND_CHUNKS = 9
REL_CLIP = 128
DEEPNORM_ALPHA = (2 * DEPTH) ** 0.25
IN_SIZES = (GROUP_WIDTH, GROUP_WIDTH, GROUP_WIDTH, 3 * GROUP_WIDTH, N_HEADS, N_HEADS,
            MLA_Q_RANK, MLA_KV_RANK + MLA_ROPE, GROUP_WIDTH, GROUP_WIDTH, GROUP_WIDTH, D_MODEL)

LANE = 128
IN_COLS = sum(IN_SIZES)
(OFF_DIFF_Q, OFF_DIFF_K, OFF_DIFF_V, OFF_GDN_QKV, OFF_GDN_A, OFF_GDN_B, OFF_MLA_Q, OFF_MLA_KV,
 OFF_BAND_Q, OFF_BAND_K, OFF_BAND_V, OFF_GATE) = (sum(IN_SIZES[:n]) for n in range(len(IN_SIZES)))

FRONT_TILE = 768
FRONT_COLS = 10 * FRONT_TILE
BACK_TILE = 512
BACK_SHIFT = OFF_BAND_Q % LANE
BACK_COLS = IN_COLS - OFF_BAND_Q
NEXT_ROWS = LANE
MLA_LANE_OFF = OFF_MLA_Q % LANE
assert OFF_GDN_QKV % LANE == 0 and OFF_GDN_A % LANE == 0 and OFF_GDN_B == OFF_GDN_A + N_HEADS
assert OFF_MLA_KV % LANE == MLA_LANE_OFF and OFF_GATE % LANE == BACK_SHIFT and BACK_COLS % BACK_TILE == 0
assert FRONT_COLS >= OFF_BAND_Q and (OFF_BAND_Q - BACK_SHIFT) % BACK_TILE == 0
COL_DIFF_Q = OFF_DIFF_Q // LANE
COL_DIFF_K = OFF_DIFF_K // LANE
COL_DIFF_V = OFF_DIFF_V // LANE
COL_GDN_QKV = OFF_GDN_QKV // LANE
COL_GDN_AB = OFF_GDN_A // LANE
COL_MLA = OFF_MLA_Q // LANE
COL_BAND_Q = 0
COL_BAND_K = (OFF_BAND_K - OFF_BAND_Q) // LANE
COL_BAND_V = (OFF_BAND_V - OFF_BAND_Q) // LANE
COL_GATE = (OFF_GATE - OFF_BAND_Q) // LANE
MLA_Q_BLOCKS = MLA_Q_RANK // LANE + 1
MLA_KV_BLOCKS = 3

ATT_BLOCK = 256
GDN_GROUP = 256
NEG_BIG = -1e30
VMEM_LIMIT = 48 * 1024 * 1024


def _params(*sem):
    return pltpu.CompilerParams(dimension_semantics=sem, vmem_limit_bytes=VMEM_LIMIT)


def _dot(a, b):
    return jnp.dot(a, b, preferred_element_type=F32)


def _dot_nt(a, b):
    return lax.dot_general(a, b, (((1,), (1,)), ((), ())), preferred_element_type=F32)


def _silu(x):
    return x * (1.0 / (1.0 + jnp.exp(-x)))


CAST_ROWS = 16


def _cast_rows(src_ref, src_row, dst_ref, dst_row, n_rows):
    def body(r, carry):
        src = pl.ds(pl.multiple_of(src_row + r * CAST_ROWS, CAST_ROWS), CAST_ROWS)
        dst = pl.ds(pl.multiple_of(dst_row + r * CAST_ROWS, CAST_ROWS), CAST_ROWS)
        dst_ref[dst, :] = src_ref[src, :].astype(BF16)
        return carry
    lax.fori_loop(0, n_rows // CAST_ROWS, body, 0)


def _in_proj_front_kernel(x_ref, w_ref, o_ref, w16_ref):
    @pl.when(pl.program_id(1) == 0)
    def _():
        _cast_rows(w_ref, 0, w16_ref, 0, w_ref.shape[0])

    o_ref[...] = _dot_nt(x_ref[...], w16_ref[...]).astype(o_ref.dtype)


def _in_proj_back_kernel(x_ref, w_ref, w_next_ref, o_ref, w16_ref):
    @pl.when(pl.program_id(1) == 0)
    def _():
        own = w_ref.shape[0] - BACK_SHIFT
        _cast_rows(w_ref, BACK_SHIFT, w16_ref, 0, own)
        _cast_rows(w_next_ref, 0, w16_ref, own, BACK_SHIFT)

    o_ref[...] = _dot_nt(x_ref[...], w16_ref[...]).astype(o_ref.dtype)


def _in_proj(xb, w_t, layer):
    m = xb.shape[0]
    tm = 512
    x_spec = pl.BlockSpec((tm, D_MODEL), lambda j, i: (i, 0))
    front = pl.pallas_call(
        _in_proj_front_kernel,
        grid=(FRONT_COLS // FRONT_TILE, m // tm),
        in_specs=[x_spec, pl.BlockSpec((None, FRONT_TILE, D_MODEL), lambda j, i: (layer, j, 0))],
        out_specs=pl.BlockSpec((tm, FRONT_TILE), lambda j, i: (i, j)),
        out_shape=jax.ShapeDtypeStruct((m, FRONT_COLS), BF16),
        scratch_shapes=[pltpu.VMEM((FRONT_TILE, D_MODEL), BF16)],
        compiler_params=_params("arbitrary", "arbitrary"),
        name="in_proj_front",
    )(xb, w_t)
    first_tile = (OFF_BAND_Q - BACK_SHIFT) // BACK_TILE
    next_blocks = BACK_TILE // NEXT_ROWS
    back = pl.pallas_call(
        _in_proj_back_kernel,
        grid=(BACK_COLS // BACK_TILE, m // tm),
        in_specs=[x_spec,
                  pl.BlockSpec((None, BACK_TILE, D_MODEL), lambda j, i: (layer, first_tile + j, 0)),
                  pl.BlockSpec((None, NEXT_ROWS, D_MODEL), lambda j, i: (layer, (first_tile + j + 1) * next_blocks, 0))],
        out_specs=pl.BlockSpec((tm, BACK_TILE), lambda j, i: (i, j)),
        out_shape=jax.ShapeDtypeStruct((m, BACK_COLS), BF16),
        scratch_shapes=[pltpu.VMEM((BACK_TILE, D_MODEL), BF16)],
        compiler_params=_params("arbitrary", "arbitrary"),
        name="in_proj_back",
    )(xb, w_t, w_t)
    return front, back


def _softmax_stats(m, l, s, scale, bias, shift):
    t = s if scale == 1.0 else s * scale
    if bias is not None:
        t = t + bias
    m_new = jnp.maximum(m, jnp.max(t, axis=0, keepdims=True) + shift)
    alpha = jnp.exp(m - m_new)
    p = jnp.exp(t - (m_new - shift))
    return m_new, alpha * l + jnp.sum(p, axis=0, keepdims=True), alpha, p.astype(BF16)


ATT_HEADS_PER_STEP = 2


def _load_values_transposed(step, v_ref, vt_ref):
    @pl.when(step == 0)
    def _():
        for h in range(vt_ref.shape[0]):
            for j in range(vt_ref.shape[1]):
                v = v_ref[ATT_BLOCK * j:ATT_BLOCK * (j + 1), HEAD_DIM * h:HEAD_DIM * (h + 1)]
                vt_ref[h, j] = v.astype(F32).T.astype(BF16)


def _attn_sweep(qs, k_ref, vt_ref, lo, i, off_scores, diag_scores):
    heads = range(len(qs))
    nq, dk = qs[0].shape

    def keys(kb, h):
        return k_ref[pl.ds(pl.multiple_of(kb * ATT_BLOCK, ATT_BLOCK), ATT_BLOCK), dk * h:dk * (h + 1)]

    def body(kb, carry):
        s_next = [_dot_nt(keys(kb + 1, h), qs[h]) for h in heads]
        prev = jnp.maximum(kb - 1, lo)
        acc = [carry[h][5] * carry[h][2] + _dot(vt_ref[h, prev], carry[h][4]) for h in heads]
        stats = [_softmax_stats(carry[h][0], carry[h][1], carry[h][3], *off_scores(kb, h)) for h in heads]
        return tuple((stats[h][0], stats[h][1], acc[h], s_next[h], stats[h][3], stats[h][2]) for h in heads)

    init = tuple((jnp.full((1, nq), NEG_BIG, F32), jnp.zeros((1, nq), F32), jnp.zeros((HEAD_DIM, nq), F32),
                  _dot_nt(keys(lo, h), qs[h]), jnp.zeros((ATT_BLOCK, nq), BF16), jnp.ones((1, nq), F32))
                 for h in heads)
    carry = lax.fori_loop(lo, i, body, init)
    prev = jnp.maximum(i - 1, lo)
    acc = [carry[h][5] * carry[h][2] + _dot(vt_ref[h, prev], carry[h][4]) for h in heads]
    stats = [_softmax_stats(carry[h][0], carry[h][1], carry[h][3], *diag_scores(h), 0.0) for h in heads]
    return [(stats[h][2] * acc[h] + _dot(vt_ref[h, i], stats[h][3]), stats[h][1]) for h in heads]


def _block_iotas():
    key = lax.broadcasted_iota(jnp.int32, (ATT_BLOCK, ATT_BLOCK), 0)
    qry = lax.broadcasted_iota(jnp.int32, (ATT_BLOCK, ATT_BLOCK), 1)
    return key, qry


def _diff_attn_kernel(lam_init, q_ref, k_ref, v_ref, gate_ref, slope_ref, lam_ref, nw_ref, o_ref, vt_ref):
    i = pl.program_id(2)
    heads = range(ATT_HEADS_PER_STEP)
    cols = [slice(HEAD_DIM * h, HEAD_DIM * (h + 1)) for h in heads]
    key, qry = _block_iotas()
    dist = (qry - key).astype(F32)
    visible = (key // CHUNK) <= (qry // CHUNK)
    qs, slopes, bias_off, bias_diag = [], [], [], []
    for h in heads:
        q = q_ref[:, cols[h]] * (DIFF_HALF ** -0.5)
        lane = lax.broadcasted_iota(jnp.int32, q.shape, 1)
        qs.append(jnp.concatenate([jnp.where(lane < DIFF_HALF, q, jnp.zeros_like(q)),
                                   jnp.where(lane >= DIFF_HALF, q, jnp.zeros_like(q))], axis=0))
        slope = slope_ref[h][:, 0:1]
        slopes.append(slope)
        off = -slope * dist
        diag = jnp.where(visible, -slope * jnp.abs(dist), NEG_BIG)
        bias_off.append(jnp.concatenate([off, off], axis=1))
        bias_diag.append(jnp.concatenate([diag, diag], axis=1))

    def off_scores(kb, h):
        return 1.0, bias_off[h], -slopes[h] * ((i - kb) * ATT_BLOCK).astype(F32)

    def diag_scores(h):
        return 1.0, bias_diag[h]

    _load_values_transposed(i, v_ref, vt_ref)
    outs = _attn_sweep(qs, k_ref, vt_ref, 0, i, off_scores, diag_scores)
    lp = lam_ref[...]
    lam = (jnp.exp(jnp.sum(lp[0:1] * lp[1:2], axis=-1, keepdims=True))
           - jnp.exp(jnp.sum(lp[2:3] * lp[3:4], axis=-1, keepdims=True)) + lam_init)
    for h in heads:
        acc, l = outs[h]
        o = acc / l
        o = (o[:, :ATT_BLOCK] - lam * o[:, ATT_BLOCK:]).T
        o = o * lax.rsqrt(jnp.mean(o * o, axis=-1, keepdims=True) + 1e-6) * nw_ref[...]
        o = o * (1.0 - lam_init)
        o_ref[:, cols[h]] = (o * _silu(gate_ref[:, cols[h]].astype(F32))).astype(o_ref.dtype)


def _diff_attention(front, back, slopes, lam_p, norm_w, lam_init, b, t):
    nq = t // ATT_BLOCK
    m = b * t
    hp = ATT_HEADS_PER_STEP
    assert COL_DIFF_Q % hp == 0 and COL_DIFF_K % hp == 0 and COL_DIFF_V % hp == 0 and COL_GATE % hp == 0
    blk = (ATT_BLOCK, hp * HEAD_DIM)
    return pl.pallas_call(
        functools.partial(_diff_attn_kernel, lam_init),
        grid=(b, N_HEADS // hp, nq),
        in_specs=[pl.BlockSpec(blk, lambda bi, h, i: (bi * nq + i, COL_DIFF_Q // hp + h)),
                  pl.BlockSpec((t, hp * HEAD_DIM), lambda bi, h, i: (bi, COL_DIFF_K // hp + h)),
                  pl.BlockSpec((t, hp * HEAD_DIM), lambda bi, h, i: (bi, COL_DIFF_V // hp + h)),
                  pl.BlockSpec(blk, lambda bi, h, i: (bi * nq + i, COL_GATE // hp + h)),
                  pl.BlockSpec((hp, 1, LANE), lambda bi, h, i: (h, 0, 0)),
                  pl.BlockSpec((4, LANE), lambda bi, h, i: (0, 0)),
                  pl.BlockSpec((1, HEAD_DIM), lambda bi, h, i: (0, 0))],
        out_specs=pl.BlockSpec(blk, lambda bi, h, i: (bi * nq + i, h)),
        out_shape=jax.ShapeDtypeStruct((m, GROUP_WIDTH), BF16),
        scratch_shapes=[_values_scratch(t)],
        compiler_params=_params("arbitrary", "arbitrary", "arbitrary"),
        name="diff_attention",
    )(front, front, front, back, slopes, lam_p, norm_w)


def _mla_attn_kernel(q_ref, k_ref, v_ref, gate_ref, o_ref, vt_ref):
    i = pl.program_id(2)
    key, qry = _block_iotas()
    mask_diag = jnp.where((key // CHUNK) <= (qry // CHUNK), 0.0, NEG_BIG)
    scale = (MLA_NOPE + MLA_ROPE) ** -0.5

    def off_scores(kb, h):
        return scale, None, 0.0

    def diag_scores(h):
        return scale, mask_diag

    _load_values_transposed(i, v_ref, vt_ref)
    dk = 2 * LANE
    qs = [q_ref[:, dk * h:dk * (h + 1)] for h in range(ATT_HEADS_PER_STEP)]
    outs = _attn_sweep(qs, k_ref, vt_ref, 0, i, off_scores, diag_scores)
    _store_gated(outs, gate_ref, o_ref)


def _store_gated(outs, gate_ref, o_ref):
    for h, (acc, l) in enumerate(outs):
        cols = slice(HEAD_DIM * h, HEAD_DIM * (h + 1))
        o_ref[:, cols] = ((acc / l).T * _silu(gate_ref[:, cols].astype(F32))).astype(o_ref.dtype)


def _mla_attention(qf, kf, vv, back, b, t):
    nq = t // ATT_BLOCK
    m = b * t
    hp = ATT_HEADS_PER_STEP
    assert (COL_GATE + 2 * N_HEADS) % hp == 0
    return pl.pallas_call(
        _mla_attn_kernel,
        grid=(b, N_HEADS // hp, nq),
        in_specs=[pl.BlockSpec((ATT_BLOCK, hp * 2 * LANE), lambda bi, h, i: (bi * nq + i, h)),
                  pl.BlockSpec((t, hp * 2 * LANE), lambda bi, h, i: (bi, h)),
                  pl.BlockSpec((t, hp * HEAD_DIM), lambda bi, h, i: (bi, h)),
                  pl.BlockSpec((ATT_BLOCK, hp * HEAD_DIM),
                               lambda bi, h, i: (bi * nq + i, (COL_GATE + 2 * N_HEADS) // hp + h))],
        out_specs=pl.BlockSpec((ATT_BLOCK, hp * HEAD_DIM), lambda bi, h, i: (bi * nq + i, h)),
        out_shape=jax.ShapeDtypeStruct((m, GROUP_WIDTH), BF16),
        scratch_shapes=[_values_scratch(t)],
        compiler_params=_params("arbitrary", "arbitrary", "arbitrary"),
        name="mla_attention",
    )(qf, kf, vv, back)


BAND_KEY_BLOCKS = (BAND_CHUNKS - 1) * CHUNK // ATT_BLOCK + 1


def _values_scratch(t):
    return pltpu.VMEM((ATT_HEADS_PER_STEP, t // ATT_BLOCK, HEAD_DIM, ATT_BLOCK), BF16)


def _band_attn_kernel(q_ref, k_ref, v_ref, gate_ref, bias_ref, o_ref, vt_ref):
    i = pl.program_id(2)
    scale = HEAD_DIM ** -0.5

    def off_scores(kb, h):
        return scale, bias_ref[h, kb - i + (BAND_KEY_BLOCKS - 1)], 0.0

    def diag_scores(h):
        return scale, bias_ref[h, BAND_KEY_BLOCKS - 1]

    lo = jnp.maximum(i - (BAND_KEY_BLOCKS - 1), 0)
    _load_values_transposed(i, v_ref, vt_ref)
    qs = [q_ref[:, HEAD_DIM * h:HEAD_DIM * (h + 1)] for h in range(ATT_HEADS_PER_STEP)]
    outs = _attn_sweep(qs, k_ref, vt_ref, lo, i, off_scores, diag_scores)
    _store_gated(outs, gate_ref, o_ref)


def _band_attention(back, bias_tiles, b, t):
    nq = t // ATT_BLOCK
    m = b * t
    hp = ATT_HEADS_PER_STEP
    assert COL_BAND_Q % hp == 0 and COL_BAND_K % hp == 0 and COL_BAND_V % hp == 0
    assert (COL_GATE + 3 * N_HEADS) % hp == 0
    blk = (ATT_BLOCK, hp * HEAD_DIM)
    return pl.pallas_call(
        _band_attn_kernel,
        grid=(b, N_HEADS // hp, nq),
        in_specs=[pl.BlockSpec(blk, lambda bi, h, i: (bi * nq + i, COL_BAND_Q // hp + h)),
                  pl.BlockSpec((t, hp * HEAD_DIM), lambda bi, h, i: (bi, COL_BAND_K // hp + h)),
                  pl.BlockSpec((t, hp * HEAD_DIM), lambda bi, h, i: (bi, COL_BAND_V // hp + h)),
                  pl.BlockSpec(blk, lambda bi, h, i: (bi * nq + i, (COL_GATE + 3 * N_HEADS) // hp + h)),
                  pl.BlockSpec((hp, BAND_KEY_BLOCKS, ATT_BLOCK, ATT_BLOCK), lambda bi, h, i: (h, 0, 0, 0))],
        out_specs=pl.BlockSpec(blk, lambda bi, h, i: (bi * nq + i, h)),
        out_shape=jax.ShapeDtypeStruct((m, GROUP_WIDTH), BF16),
        scratch_shapes=[_values_scratch(t)],
        compiler_params=_params("arbitrary", "arbitrary", "arbitrary"),
        name="band_attention",
    )(back, back, back, back, bias_tiles)


def _band_bias_tiles(rel_bias):
    n = ATT_BLOCK
    back = (BAND_KEY_BLOCKS - 1) - jnp.arange(BAND_KEY_BLOCKS)
    j = jnp.arange(2 * n)
    q_minus_k = jnp.where(j < n, j, j - 2 * n)
    rel = back[:, None] * n + q_minus_k[None, :]
    vals = rel_bias.astype(F32)[:, jnp.clip(rel, -REL_CLIP, REL_CLIP) + REL_CLIP]
    flat = jnp.tile(vals, (1, 1, n))[:, :, :n * (2 * n - 1)]
    tiles = flat.reshape(N_HEADS, BAND_KEY_BLOCKS, n, 2 * n - 1)[:, :, :, :n]
    key = jnp.arange(n)[None, :, None]
    qry = jnp.arange(n)[None, None, :]
    chunk_back = back[:, None, None] * (n // CHUNK) + qry // CHUNK - key // CHUNK
    visible = (chunk_back >= 0) & (chunk_back < BAND_CHUNKS)
    return jnp.where(visible[None], tiles, NEG_BIG)


def _mla_prep_kernel(cdq_ref, ckv_ref, qg_ref, kvg_ref, w1_ref, w2_ref, wk_ref, wv_ref, ct_ref, st_ref,
                     q_out, k_out, v_out):
    ct = ct_ref[...]
    st = st_ref[...]
    ckv = ckv_ref[...].astype(F32)
    cq = jnp.concatenate([cdq_ref[...].astype(F32), ckv[:, :LANE]], axis=1)
    lane_q = lax.broadcasted_iota(jnp.int32, (1, cq.shape[1]), 1)
    cq = jnp.where((lane_q >= MLA_LANE_OFF) & (lane_q < MLA_LANE_OFF + MLA_Q_RANK), cq, 0.0)
    ms_q = jnp.sum(cq * cq, axis=-1, keepdims=True) * (1.0 / MLA_Q_RANK)
    nq = (cq * lax.rsqrt(ms_q + 1e-6) * qg_ref[...]).astype(BF16)
    q1 = _dot(nq, w1_ref[...])
    q2 = _dot(nq, w2_ref[...])
    lane_kv = lax.broadcasted_iota(jnp.int32, (1, ckv.shape[1]), 1)
    lat = jnp.where((lane_kv >= MLA_LANE_OFF) & (lane_kv < MLA_LANE_OFF + MLA_KV_RANK), ckv, 0.0)
    ms_kv = jnp.sum(lat * lat, axis=-1, keepdims=True) * (1.0 / MLA_KV_RANK)
    nkv = (lat * lax.rsqrt(ms_kv + 1e-6) * kvg_ref[...]).astype(BF16)
    kn = _dot(nkv, wk_ref[...])
    v_out[...] = _dot(nkv, wv_ref[...]).astype(v_out.dtype)
    kr = ckv[:, 2 * LANE:]
    half = MLA_ROPE // 2
    lane_r = lax.broadcasted_iota(jnp.int32, (1, LANE), 1)
    rot = jnp.where(lane_r < MLA_LANE_OFF + half, -pltpu.roll(kr, LANE - half, axis=1), pltpu.roll(kr, half, axis=1))
    kr = (kr * ct + rot * st).astype(k_out.dtype)
    for h in range(N_HEADS):
        a, bq = 2 * LANE * h, LANE * h
        q_out[:, a:a + LANE] = q1[:, a:a + LANE].astype(q_out.dtype)
        q_out[:, a + LANE:a + 2 * LANE] = (q1[:, a + LANE:a + 2 * LANE] * ct
                                           + q2[:, bq:bq + LANE] * st).astype(q_out.dtype)
        k_out[:, a:a + LANE] = kn[:, bq:bq + LANE].astype(k_out.dtype)
        k_out[:, a + LANE:a + 2 * LANE] = kr


def _mla_prep(front, q_gain, kv_gain, w1, w2, wk, wv, cos_t, sin_t, b, t):
    m = b * t
    tm = 256
    nt = t // tm
    full = lambda shape: pl.BlockSpec(shape, lambda i: (0, 0))
    q_blocks = MLA_Q_BLOCKS - 1
    assert COL_MLA % q_blocks == 0 and (COL_MLA + q_blocks) % MLA_KV_BLOCKS == 0
    return pl.pallas_call(
        _mla_prep_kernel,
        grid=(m // tm,),
        in_specs=[pl.BlockSpec((tm, q_blocks * LANE), lambda i: (i, COL_MLA // q_blocks)),
                  pl.BlockSpec((tm, MLA_KV_BLOCKS * LANE), lambda i: (i, (COL_MLA + q_blocks) // MLA_KV_BLOCKS)),
                  full(q_gain.shape), full(kv_gain.shape),
                  full(w1.shape), full(w2.shape), full(wk.shape), full(wv.shape),
                  pl.BlockSpec((tm, LANE), lambda i: (i % nt, 0)),
                  pl.BlockSpec((tm, LANE), lambda i: (i % nt, 0))],
        out_specs=[pl.BlockSpec((tm, 2 * GROUP_WIDTH), lambda i: (i, 0)),
                   pl.BlockSpec((tm, 2 * GROUP_WIDTH), lambda i: (i, 0)),
                   pl.BlockSpec((tm, GROUP_WIDTH), lambda i: (i, 0))],
        out_shape=[jax.ShapeDtypeStruct((m, 2 * GROUP_WIDTH), BF16),
                   jax.ShapeDtypeStruct((m, 2 * GROUP_WIDTH), BF16),
                   jax.ShapeDtypeStruct((m, GROUP_WIDTH), BF16)],
        compiler_params=_params("arbitrary"),
        name="mla_prep",
    )(front, front, q_gain, kv_gain, w1, w2, wk, wv, cos_t, sin_t)


HALO = 16


def _gdn_prep_kernel(blocks_per_seq, x_ref, halo_ref, ab_ref, cw_ref, alog_ref, dtb_ref,
                     qn_ref, kn_ref, v_ref, gcol_ref, bcol_ref, grow_ref):
    i = pl.program_id(0)
    halo_scale = jnp.where(i % blocks_per_seq == 0, 0.0, 1.0)
    outs = (qn_ref, kn_ref, v_ref)
    for j in range(3 * N_HEADS):
        sl = slice(LANE * j, LANE * (j + 1))
        cat = jnp.concatenate([halo_ref[:, sl].astype(F32) * halo_scale, x_ref[:, sl].astype(F32)], axis=0)
        w = cw_ref[:, sl]
        acc = cat * w[GDN_CONV - 1:GDN_CONV]
        for s in range(1, GDN_CONV):
            acc = acc + pltpu.roll(cat, s, axis=0) * w[GDN_CONV - 1 - s:GDN_CONV - s]
        y = _silu(acc[HALO:])
        part, head = divmod(j, N_HEADS)
        if part < 2:
            y = y * lax.rsqrt(jnp.sum(y * y, axis=-1, keepdims=True) + 1e-6)
        if part == 0:
            y = y * HEAD_DIM ** -0.5
        outs[part][:, LANE * head:LANE * (head + 1)] = y.astype(BF16)

    ab = ab_ref[...].astype(F32)
    z = ab + dtb_ref[...]
    softplus = jnp.maximum(z, 0.0) + jnp.log1p(jnp.exp(-jnp.abs(z)))
    g = -jnp.exp(alog_ref[...]) * softplus
    beta = 1.0 / (1.0 + jnp.exp(-ab))
    row = lax.broadcasted_iota(jnp.int32, g.shape, 0) % CHUNK
    s = 1
    while s < CHUNK:
        g = g + jnp.where(row >= s, pltpu.roll(g, s, axis=0), 0.0)
        s *= 2
    gt = g.T
    for h in range(N_HEADS):
        gcol_ref[h] = jnp.broadcast_to(g[:, h:h + 1], g.shape)
        bcol_ref[h] = jnp.broadcast_to(beta[:, N_HEADS + h:N_HEADS + h + 1], g.shape)
        grow_ref[h] = gt[h:h + 1, :]


def _gdn_prep(front, conv_w, alog, dtb, b, t):
    m = b * t
    tm = GDN_GROUP
    nt = t // tm
    assert OFF_GDN_QKV % (3 * GROUP_WIDTH) == 0
    qkv_block = OFF_GDN_QKV // (3 * GROUP_WIDTH)
    return pl.pallas_call(
        functools.partial(_gdn_prep_kernel, nt),
        grid=(m // tm,),
        in_specs=[pl.BlockSpec((tm, 3 * GROUP_WIDTH), lambda i: (i, qkv_block)),
                  pl.BlockSpec((HALO, 3 * GROUP_WIDTH), lambda i: (jnp.maximum(i * (tm // HALO) - 1, 0), qkv_block)),
                  pl.BlockSpec((tm, LANE), lambda i: (i, COL_GDN_AB)),
                  pl.BlockSpec((GDN_CONV, 3 * GROUP_WIDTH), lambda i: (0, 0)),
                  pl.BlockSpec((1, LANE), lambda i: (0, 0)),
                  pl.BlockSpec((1, LANE), lambda i: (0, 0))],
        out_specs=[pl.BlockSpec((tm, GROUP_WIDTH), lambda i: (i, 0)),
                   pl.BlockSpec((tm, GROUP_WIDTH), lambda i: (i, 0)),
                   pl.BlockSpec((tm, GROUP_WIDTH), lambda i: (i, 0)),
                   pl.BlockSpec((None, N_HEADS, tm, LANE), lambda i: (i // nt, 0, i % nt, 0)),
                   pl.BlockSpec((None, N_HEADS, tm, LANE), lambda i: (i // nt, 0, i % nt, 0)),
                   pl.BlockSpec((None, N_HEADS, 1, tm), lambda i: (i // nt, 0, 0, i % nt))],
        out_shape=[jax.ShapeDtypeStruct((m, GROUP_WIDTH), BF16),
                   jax.ShapeDtypeStruct((m, GROUP_WIDTH), BF16),
                   jax.ShapeDtypeStruct((m, GROUP_WIDTH), BF16),
                   jax.ShapeDtypeStruct((b, N_HEADS, t, LANE), F32),
                   jax.ShapeDtypeStruct((b, N_HEADS, t, LANE), F32),
                   jax.ShapeDtypeStruct((b, N_HEADS, 1, t), F32)],
        compiler_params=_params("arbitrary"),
        name="gdn_prep",
    )(front, front, front, conv_w, alog, dtb)


GDN_HEADS_PER_STEP = 4


def _gdn_kernel(q_ref, k_ref, v_ref, gc_ref, bc_ref, gr_ref, gate_ref, nw_ref, o_ref, state_ref):
    n = pl.program_id(2)

    @pl.when(n == 0)
    def _():
        state_ref[...] = jnp.zeros_like(state_ref)

    g_sz = GDN_GROUP
    ri = lax.broadcasted_iota(jnp.int32, (g_sz, g_sz), 0)
    ci = lax.broadcasted_iota(jnp.int32, (g_sz, g_sz), 1)
    dif = jnp.where((ri // CHUNK) == (ci // CHUNK), ri - ci, -1)
    heads = range(GDN_HEADS_PER_STEP)
    cols = [slice(HEAD_DIM * hb, HEAD_DIM * (hb + 1)) for hb in heads]
    outs = _gdn_heads(dif, [q_ref[:, sl] for sl in cols], [k_ref[:, sl] for sl in cols],
                      [v_ref[:, sl] for sl in cols], [gc_ref[hb] for hb in heads], [bc_ref[hb] for hb in heads],
                      [gr_ref[hb] for hb in heads], state_ref)
    for hb in heads:
        o = outs[hb]
        o = o * lax.rsqrt(jnp.mean(o * o, axis=-1, keepdims=True) + 1e-6) * nw_ref[...]
        o_ref[:, cols[hb]] = (o * _silu(gate_ref[:, cols[hb]].astype(F32))).astype(o_ref.dtype)


def _gdn_heads(dif, q16, k16, v16, gc, beta, gr, state_ref):
    heads = range(len(q16))
    n_chunks = GDN_GROUP // CHUNK
    incl = dif >= 0
    kf = [k16[h].astype(F32) for h in heads]
    eg = [jnp.exp(gc[h]) for h in heads]
    kbeta = [kf[h] * beta[h] for h in heads]
    decay = [jnp.where(incl, jnp.exp(jnp.where(incl, jnp.concatenate([gc[h], gc[h]], axis=1) - gr[h], 0.0)), 0.0)
             for h in heads]

    kk = [_dot_nt(kbeta[h].astype(BF16), k16[h]) for h in heads]
    mpow = [jnp.where(dif > 0, -(kk[h] * decay[h]), 0.0) for h in heads]
    inv = [jnp.where(dif == 0, 1.0, 0.0) + mpow[h] for h in heads]
    mpow16 = [mpow[h].astype(BF16) for h in heads]
    for _ in range(int(math.log2(CHUNK)) - 1):
        mpow16 = [_dot(mpow16[h], mpow16[h]).astype(BF16) for h in heads]
        inv = [inv[h] + _dot(inv[h].astype(BF16), mpow16[h]) for h in heads]

    rhs = [jnp.concatenate([v16[h].astype(F32) * beta[h], kbeta[h] * eg[h]], axis=1).astype(BF16) for h in heads]
    uw = [_dot(inv[h].astype(BF16), rhs[h]) for h in heads]
    u = [uw[h][:, :HEAD_DIM] for h in heads]
    w = [uw[h][:, HEAD_DIM:].astype(BF16) for h in heads]
    qk = [(_dot_nt(q16[h], k16[h]) * decay[h]).astype(BF16) for h in heads]
    qe = [(q16[h].astype(F32) * eg[h]).astype(BF16) for h in heads]
    g_last_rows = [[gc[h][CHUNK * (c + 1) - 1:CHUNK * (c + 1), :] for c in range(n_chunks)] for h in heads]
    kdec_t = [(kf[h] * jnp.exp(jnp.concatenate([jnp.broadcast_to(g, (CHUNK, LANE)) for g in g_last_rows[h]], axis=0)
                               - gc[h])).T.astype(BF16) for h in heads]
    col_chunk = lax.broadcasted_iota(jnp.int32, kdec_t[0].shape, 1) // CHUNK

    vparts = [[u[h][CHUNK * c:CHUNK * (c + 1)] for c in range(n_chunks)] for h in heads]
    state = [state_ref[h] for h in heads]
    outs = [[] for _ in heads]
    for c in range(n_chunks):
        rows = slice(CHUNK * c, CHUNK * (c + 1))
        s16 = [state[h].astype(BF16) for h in heads]
        ws = [_dot(w[h][rows], s16[h]) for h in heads]
        for h in heads:
            vparts[h][c] = u[h][rows] - ws[h]
        vfull = [jnp.concatenate(vparts[h], axis=0).astype(BF16) for h in heads]
        upd = [_dot(jnp.where(col_chunk == c, kdec_t[h], jnp.zeros_like(kdec_t[h])), vfull[h]) for h in heads]
        state = [state[h] * jnp.exp(g_last_rows[h][c]) + upd[h] for h in heads]
        for h in heads:
            outs[h].append(_dot(qe[h][rows], s16[h]) + _dot(qk[h][rows], vfull[h]))
    for h in heads:
        state_ref[h] = state[h]
    return [jnp.concatenate(outs[h], axis=0) for h in heads]


def _gated_delta_net(qn, kn, vv, gcol, bcol, grow, back, norm_w, b, t):
    assert (COL_GATE + N_HEADS) % GDN_HEADS_PER_STEP == 0
    m = b * t
    ng = t // GDN_GROUP
    hb = GDN_HEADS_PER_STEP
    blk = (GDN_GROUP, HEAD_DIM * hb)
    tok = lambda bi, h, n: (bi * ng + n, h)
    return pl.pallas_call(
        _gdn_kernel,
        grid=(b, N_HEADS // hb, ng),
        in_specs=[pl.BlockSpec(blk, tok), pl.BlockSpec(blk, tok), pl.BlockSpec(blk, tok),
                  pl.BlockSpec((None, hb, GDN_GROUP, LANE), lambda bi, h, n: (bi, h, n, 0)),
                  pl.BlockSpec((None, hb, GDN_GROUP, LANE), lambda bi, h, n: (bi, h, n, 0)),
                  pl.BlockSpec((None, hb, 1, GDN_GROUP), lambda bi, h, n: (bi, h, 0, n)),
                  pl.BlockSpec(blk, lambda bi, h, n: (bi * ng + n, (COL_GATE + N_HEADS) // hb + h)),
                  pl.BlockSpec((1, HEAD_DIM), lambda bi, h, n: (0, 0))],
        out_specs=pl.BlockSpec(blk, tok),
        out_shape=jax.ShapeDtypeStruct((m, GROUP_WIDTH), BF16),
        scratch_shapes=[pltpu.VMEM((hb, HEAD_DIM, HEAD_DIM), F32)],
        compiler_params=_params("arbitrary", "arbitrary", "arbitrary"),
        name="gated_delta_net",
    )(qn, kn, vv, gcol, bcol, grow, back, norm_w)


def _out_proj_kernel(oa_ref, ob_ref, oc_ref, od_ref, w_ref, x_ref, g_ref, b_ref, y_ref, y16_ref):
    o = jnp.concatenate([oa_ref[...], ob_ref[...], oc_ref[...], od_ref[...]], axis=1)
    z = _dot(o, w_ref[...]) + DEEPNORM_ALPHA * x_ref[...]
    zc = z - jnp.mean(z, axis=-1, keepdims=True)
    y = zc * lax.rsqrt(jnp.mean(zc * zc, axis=-1, keepdims=True) + 1e-5) * g_ref[...] + b_ref[...]
    y_ref[...] = y
    y16_ref[...] = y.astype(BF16)


def _out_proj_ln(oa, ob, oc, od, w16, x2d, gain, bias):
    m = x2d.shape[0]
    tm = 128
    grp = pl.BlockSpec((tm, GROUP_WIDTH), lambda i: (i, 0))
    row = pl.BlockSpec((tm, D_MODEL), lambda i: (i, 0))
    vec = pl.BlockSpec((1, D_MODEL), lambda i: (0, 0))
    weight = pl.BlockSpec((D_MODEL, D_MODEL), lambda i: (0, 0), pipeline_mode=pl.Buffered(1))
    return pl.pallas_call(
        _out_proj_kernel,
        grid=(m // tm,),
        in_specs=[grp, grp, grp, grp, weight, row, vec, vec],
        out_specs=[row, row],
        out_shape=[jax.ShapeDtypeStruct((m, D_MODEL), F32), jax.ShapeDtypeStruct((m, D_MODEL), BF16)],
        compiler_params=_params("arbitrary"),
        name="out_proj_layernorm",
    )(oa, ob, oc, od, w16, x2d, gain, bias)


def _place(a, axis, offset, size):
    pad = [(0, 0)] * a.ndim
    pad[axis] = (offset, size - offset - a.shape[axis])
    return jnp.pad(a, pad)


def _layout_w_uq(w):
    w = w.reshape(MLA_Q_RANK, N_HEADS, MLA_NOPE + MLA_ROPE)
    rope = w[:, :, MLA_NOPE:]
    rot = jnp.concatenate([-rope[:, :, MLA_ROPE // 2:], rope[:, :, :MLA_ROPE // 2]], axis=2)
    w1 = jnp.concatenate([w[:, :, :MLA_NOPE], _place(rope, 2, MLA_LANE_OFF, LANE)], axis=2)
    w1 = w1.reshape(MLA_Q_RANK, N_HEADS * 2 * LANE)
    w2 = _place(rot, 2, MLA_LANE_OFF, LANE).reshape(MLA_Q_RANK, N_HEADS * LANE)
    rows = MLA_Q_BLOCKS * LANE
    return _place(w1, 0, MLA_LANE_OFF, rows).astype(BF16), _place(w2, 0, MLA_LANE_OFF, rows).astype(BF16)


def _layout_w_ukv(w):
    w = w.reshape(MLA_KV_RANK, N_HEADS, MLA_NOPE + HEAD_DIM)
    wk = w[:, :, :MLA_NOPE].reshape(MLA_KV_RANK, GROUP_WIDTH)
    wv = w[:, :, MLA_NOPE:].reshape(MLA_KV_RANK, GROUP_WIDTH)
    rows = MLA_KV_BLOCKS * LANE
    return _place(wk, 0, MLA_LANE_OFF, rows).astype(BF16), _place(wv, 0, MLA_LANE_OFF, rows).astype(BF16)


def _rope_tables(t):
    half = MLA_ROPE // 2
    inv = ROPE_THETA ** (-jnp.arange(half, dtype=F32) / half)
    ang = jnp.arange(t).astype(F32)[:, None] * inv[None, :]
    cos_t = _place(jnp.concatenate([jnp.cos(ang), jnp.cos(ang)], axis=1), 1, MLA_LANE_OFF, LANE)
    sin_t = _place(jnp.concatenate([jnp.sin(ang), jnp.sin(ang)], axis=1), 1, MLA_LANE_OFF, LANE)
    return cos_t, sin_t


def _pad_lanes(v):
    return jnp.concatenate([v.astype(F32), jnp.zeros((LANE - v.shape[0],), F32)])[None, :]


def _layer(x2d, x16, layer_idx, b, t, w_in, diff_lambda, diff_norm, gdn_conv, gdn_a_log, gdn_dt_bias, gdn_norm,
           mla_q_norm, mla_w_uq, mla_kv_norm, mla_w_ukv, rel_bias, w_out, ln_gain, ln_bias, tables):
    front, back = _in_proj(x16, w_in, layer_idx)

    lam_init = 0.8 - 0.6 * math.exp(-0.3 * layer_idx)
    slopes = 2.0 ** (-8.0 * jnp.arange(1, N_HEADS + 1, dtype=F32) / N_HEADS)
    slopes = jnp.broadcast_to(slopes[:, None, None], (N_HEADS, 1, LANE))
    lam_p = jnp.concatenate([diff_lambda.astype(F32), jnp.zeros((4, LANE - DIFF_HALF), F32)], axis=1)
    o_a = _diff_attention(front, back, slopes, lam_p, diff_norm.astype(F32)[None, :], lam_init, b, t)

    qn, kn, vv, gcol, bcol, grow = _gdn_prep(front, gdn_conv.astype(F32), _pad_lanes(gdn_a_log),
                                              _pad_lanes(gdn_dt_bias), b, t)
    o_b = _gated_delta_net(qn, kn, vv, gcol, bcol, grow, back, gdn_norm.astype(F32)[None, :], b, t)

    w1, w2 = _layout_w_uq(mla_w_uq)
    wk, wv = _layout_w_ukv(mla_w_ukv)
    cos_t, sin_t = tables
    q_gain = _place(mla_q_norm.astype(F32)[None, :], 1, MLA_LANE_OFF, MLA_Q_BLOCKS * LANE)
    kv_gain = _place(mla_kv_norm.astype(F32)[None, :], 1, MLA_LANE_OFF, MLA_KV_BLOCKS * LANE)
    qf, kf, vc = _mla_prep(front, q_gain, kv_gain, w1, w2, wk, wv, cos_t, sin_t, b, t)
    o_c = _mla_attention(qf, kf, vc, back, b, t)

    o_d = _band_attention(back, _band_bias_tiles(rel_bias), b, t)

    return _out_proj_ln(o_a, o_b, o_c, o_d, w_out.astype(BF16), x2d,
                        ln_gain.astype(F32)[None, :], ln_bias.astype(F32)[None, :])


def kernel(x, w_in, diff_lambda, diff_norm, gdn_conv, gdn_a_log, gdn_dt_bias, gdn_norm, mla_q_norm, mla_w_uq,
           mla_kv_norm, mla_w_ukv, rel_bias, w_out, ln_gain, ln_bias):
    b, t, d = x.shape
    assert d == D_MODEL and t % GDN_GROUP == 0 and (b * t) % 512 == 0
    tables = _rope_tables(t)
    x2d = x.reshape(b * t, d)
    x16 = x2d.astype(BF16)
    w_t = jnp.swapaxes(w_in, 1, 2)
    for l in range(DEPTH):
        x2d, x16 = _layer(x2d, x16, l, b, t, w_t, diff_lambda[l], diff_norm[l], gdn_conv[l], gdn_a_log[l],
                          gdn_dt_bias[l], gdn_norm[l], mla_q_norm[l], mla_w_uq[l], mla_kv_norm[l],
                          mla_w_ukv[l], rel_bias[l], w_out[l], ln_gain[l], ln_bias[l], tables)
    return x2d.reshape(b, t, d)
```

```python
import functools
import math

import jax
import jax.numpy as jnp
from jax import lax
from jax.experimental import pallas as pl
from jax.experimental.pallas import tpu as pltpu

F32 = jnp.float32
BF16 = jnp.bfloat16

D_MODEL = 4096
DEPTH = 2
CHUNK = 64
N_HEADS = 8
HEAD_DIM = 128
GROUP_WIDTH = N_HEADS * HEAD_DIM
DIFF_HALF = HEAD_DIM // 2
GDN_CONV = 4
MLA_Q_RANK = 768
MLA_KV_RANK = 256
MLA_NOPE = 128
MLA_ROPE = 64
ROPE_THETA = 10000.0
BAND_CHUNKS = 9
REL_CLIP = 128
DEEPNORM_ALPHA = (2 * DEPTH) ** 0.25
IN_SIZES = (GROUP_WIDTH, GROUP_WIDTH, GROUP_WIDTH, 3 * GROUP_WIDTH, N_HEADS, N_HEADS,
            MLA_Q_RANK, MLA_KV_RANK + MLA_ROPE, GROUP_WIDTH, GROUP_WIDTH, GROUP_WIDTH, D_MODEL)

LANE = 128
IN_COLS = sum(IN_SIZES)
(OFF_DIFF_Q, OFF_DIFF_K, OFF_DIFF_V, OFF_GDN_QKV, OFF_GDN_A, OFF_GDN_B, OFF_MLA_Q, OFF_MLA_KV,
 OFF_BAND_Q, OFF_BAND_K, OFF_BAND_V, OFF_GATE) = (sum(IN_SIZES[:n]) for n in range(len(IN_SIZES)))

FRONT_TILE = 768
FRONT_COLS = 10 * FRONT_TILE
BACK_TILE = 512
BACK_SHIFT = OFF_BAND_Q % LANE
BACK_COLS = IN_COLS - OFF_BAND_Q
NEXT_ROWS = LANE
MLA_LANE_OFF = OFF_MLA_Q % LANE
assert OFF_GDN_QKV % LANE == 0 and OFF_GDN_A % LANE == 0 and OFF_GDN_B == OFF_GDN_A + N_HEADS
assert OFF_MLA_KV % LANE == MLA_LANE_OFF and OFF_GATE % LANE == BACK_SHIFT and BACK_COLS % BACK_TILE == 0
assert FRONT_COLS >= OFF_BAND_Q and (OFF_BAND_Q - BACK_SHIFT) % BACK_TILE == 0
COL_DIFF_Q = OFF_DIFF_Q // LANE
COL_DIFF_K = OFF_DIFF_K // LANE
COL_DIFF_V = OFF_DIFF_V // LANE
COL_GDN_QKV = OFF_GDN_QKV // LANE
COL_GDN_AB = OFF_GDN_A // LANE
COL_MLA = OFF_MLA_Q // LANE
COL_BAND_Q = 0
COL_BAND_K = (OFF_BAND_K - OFF_BAND_Q) // LANE
COL_BAND_V = (OFF_BAND_V - OFF_BAND_Q) // LANE
COL_GATE = (OFF_GATE - OFF_BAND_Q) // LANE
MLA_Q_BLOCKS = MLA_Q_RANK // LANE + 1
MLA_KV_BLOCKS = 3

ATT_BLOCK = 256
GDN_GROUP = 256
NEG_BIG = -1e30
VMEM_LIMIT = 48 * 1024 * 1024


def _params(*sem):
    return pltpu.CompilerParams(dimension_semantics=sem, vmem_limit_bytes=VMEM_LIMIT)


def _dot(a, b):
    return jnp.dot(a, b, preferred_element_type=F32)


def _dot_nt(a, b):
    return lax.dot_general(a, b, (((1,), (1,)), ((), ())), preferred_element_type=F32)


def _silu(x):
    return x * (1.0 / (1.0 + jnp.exp(-x)))


CAST_ROWS = 16


def _cast_rows(src_ref, src_row, dst_ref, dst_row, n_rows):
    def body(r, carry):
        src = pl.ds(pl.multiple_of(src_row + r * CAST_ROWS, CAST_ROWS), CAST_ROWS)
        dst = pl.ds(pl.multiple_of(dst_row + r * CAST_ROWS, CAST_ROWS), CAST_ROWS)
        dst_ref[dst, :] = src_ref[src, :].astype(BF16)
        return carry
    lax.fori_loop(0, n_rows // CAST_ROWS, body, 0)


def _in_proj_front_kernel(x_ref, w_ref, o_ref, w16_ref):
    @pl.when(pl.program_id(1) == 0)
    def _():
        _cast_rows(w_ref, 0, w16_ref, 0, w_ref.shape[0])

    o_ref[...] = _dot_nt(x_ref[...], w16_ref[...]).astype(o_ref.dtype)


def _in_proj_back_kernel(x_ref, w_ref, w_next_ref, o_ref, w16_ref):
    @pl.when(pl.program_id(1) == 0)
    def _():
        own = w_ref.shape[0] - BACK_SHIFT
        _cast_rows(w_ref, BACK_SHIFT, w16_ref, 0, own)
        _cast_rows(w_next_ref, 0, w16_ref, own, BACK_SHIFT)

    o_ref[...] = _dot_nt(x_ref[...], w16_ref[...]).astype(o_ref.dtype)


def _in_proj(xb, w_t, layer):
    m = xb.shape[0]
    tm = 512
    x_spec = pl.BlockSpec((tm, D_MODEL), lambda j, i: (i, 0))
    front = pl.pallas_call(
        _in_proj_front_kernel,
        grid=(FRONT_COLS // FRONT_TILE, m // tm),
        in_specs=[x_spec, pl.BlockSpec((None, FRONT_TILE, D_MODEL), lambda j, i: (layer, j, 0))],
        out_specs=pl.BlockSpec((tm, FRONT_TILE), lambda j, i: (i, j)),
        out_shape=jax.ShapeDtypeStruct((m, FRONT_COLS), BF16),
        scratch_shapes=[pltpu.VMEM((FRONT_TILE, D_MODEL), BF16)],
        compiler_params=_params("arbitrary", "arbitrary"),
        name="in_proj_front",
    )(xb, w_t)
    first_tile = (OFF_BAND_Q - BACK_SHIFT) // BACK_TILE
    next_blocks = BACK_TILE // NEXT_ROWS
    back = pl.pallas_call(
        _in_proj_back_kernel,
        grid=(BACK_COLS // BACK_TILE, m // tm),
        in_specs=[x_spec,
                  pl.BlockSpec((None, BACK_TILE, D_MODEL), lambda j, i: (layer, first_tile + j, 0)),
                  pl.BlockSpec((None, NEXT_ROWS, D_MODEL), lambda j, i: (layer, (first_tile + j + 1) * next_blocks, 0))],
        out_specs=pl.BlockSpec((tm, BACK_TILE), lambda j, i: (i, j)),
        out_shape=jax.ShapeDtypeStruct((m, BACK_COLS), BF16),
        scratch_shapes=[pltpu.VMEM((BACK_TILE, D_MODEL), BF16)],
        compiler_params=_params("arbitrary", "arbitrary"),
        name="in_proj_back",
    )(xb, w_t, w_t)
    return front, back


def _softmax_stats(m, l, s, scale, bias, shift):
    t = s if scale == 1.0 else s * scale
    if bias is not None:
        t = t + bias
    m_new = jnp.maximum(m, jnp.max(t, axis=0, keepdims=True) + shift)
    alpha = jnp.exp(m - m_new)
    p = jnp.exp(t - (m_new - shift))
    return m_new, alpha * l + jnp.sum(p, axis=0, keepdims=True), alpha, p.astype(BF16)


ATT_HEADS_PER_STEP = 4


def _load_values_transposed(step, v_ref, vt_ref):
    @pl.when(step == 0)
    def _():
        for h in range(vt_ref.shape[0]):
            for j in range(vt_ref.shape[1]):
                v = v_ref[ATT_BLOCK * j:ATT_BLOCK * (j + 1), HEAD_DIM * h:HEAD_DIM * (h + 1)]
                vt_ref[h, j] = v.astype(F32).T.astype(BF16)


def _attn_sweep(qs, k_ref, vt_ref, lo, i, off_scores, diag_scores):
    heads = range(len(qs))
    nq, dk = qs[0].shape

    def keys(kb, h):
        return k_ref[pl.ds(pl.multiple_of(kb * ATT_BLOCK, ATT_BLOCK), ATT_BLOCK), dk * h:dk * (h + 1)]

    def body(kb, carry):
        s_next = [_dot_nt(keys(kb + 1, h), qs[h]) for h in heads]
        prev = jnp.maximum(kb - 1, lo)
        acc = [carry[h][5] * carry[h][2] + _dot(vt_ref[h, prev], carry[h][4]) for h in heads]
        stats = [_softmax_stats(carry[h][0], carry[h][1], carry[h][3], *off_scores(kb, h)) for h in heads]
        return tuple((stats[h][0], stats[h][1], acc[h], s_next[h], stats[h][3], stats[h][2]) for h in heads)

    init = tuple((jnp.full((1, nq), NEG_BIG, F32), jnp.zeros((1, nq), F32), jnp.zeros((HEAD_DIM, nq), F32),
                  _dot_nt(keys(lo, h), qs[h]), jnp.zeros((ATT_BLOCK, nq), BF16), jnp.ones((1, nq), F32))
                 for h in heads)
    carry = lax.fori_loop(lo, i, body, init)
    prev = jnp.maximum(i - 1, lo)
    acc = [carry[h][5] * carry[h][2] + _dot(vt_ref[h, prev], carry[h][4]) for h in heads]
    stats = [_softmax_stats(carry[h][0], carry[h][1], carry[h][3], *diag_scores(h), 0.0) for h in heads]
    return [(stats[h][2] * acc[h] + _dot(vt_ref[h, i], stats[h][3]), stats[h][1]) for h in heads]


def _block_iotas():
    key = lax.broadcasted_iota(jnp.int32, (ATT_BLOCK, ATT_BLOCK), 0)
    qry = lax.broadcasted_iota(jnp.int32, (ATT_BLOCK, ATT_BLOCK), 1)
    return key, qry


def _diff_attn_kernel(lam_init, q_ref, k_ref, v_ref, gate_ref, slope_ref, lam_ref, nw_ref, o_ref, vt_ref):
    i = pl.program_id(2)
    heads = range(ATT_HEADS_PER_STEP)
    cols = [slice(HEAD_DIM * h, HEAD_DIM * (h + 1)) for h in heads]
    key, qry = _block_iotas()
    dist = (qry - key).astype(F32)
    visible = (key // CHUNK) <= (qry // CHUNK)
    qs, slopes, bias_off, bias_diag = [], [], [], []
    for h in heads:
        q = q_ref[:, cols[h]] * (DIFF_HALF ** -0.5)
        lane = lax.broadcasted_iota(jnp.int32, q.shape, 1)
        qs.append(jnp.concatenate([jnp.where(lane < DIFF_HALF, q, jnp.zeros_like(q)),
                                   jnp.where(lane >= DIFF_HALF, q, jnp.zeros_like(q))], axis=0))
        slope = slope_ref[h][:, 0:1]
        slopes.append(slope)
        off = -slope * dist
        diag = jnp.where(visible, -slope * jnp.abs(dist), NEG_BIG)
        bias_off.append(jnp.concatenate([off, off], axis=1))
        bias_diag.append(jnp.concatenate([diag, diag], axis=1))

    def off_scores(kb, h):
        return 1.0, bias_off[h], -slopes[h] * ((i - kb) * ATT_BLOCK).astype(F32)

    def diag_scores(h):
        return 1.0, bias_diag[h]

    _load_values_transposed(i, v_ref, vt_ref)
    outs = _attn_sweep(qs, k_ref, vt_ref, 0, i, off_scores, diag_scores)
    lp = lam_ref[...]
    lam = (jnp.exp(jnp.sum(lp[0:1] * lp[1:2], axis=-1, keepdims=True))
           - jnp.exp(jnp.sum(lp[2:3] * lp[3:4], axis=-1, keepdims=True)) + lam_init)
    for h in heads:
        acc, l = outs[h]
        o = acc / l
        o = (o[:, :ATT_BLOCK] - lam * o[:, ATT_BLOCK:]).T
        o = o * lax.rsqrt(jnp.mean(o * o, axis=-1, keepdims=True) + 1e-6) * nw_ref[...]
        o = o * (1.0 - lam_init)
        o_ref[:, cols[h]] = (o * _silu(gate_ref[:, cols[h]].astype(F32))).astype(o_ref.dtype)


def _diff_attention(front, back, slopes, lam_p, norm_w, lam_init, b, t):
    nq = t // ATT_BLOCK
    m = b * t
    hp = ATT_HEADS_PER_STEP
    assert COL_DIFF_Q % hp == 0 and COL_DIFF_K % hp == 0 and COL_DIFF_V % hp == 0 and COL_GATE % hp == 0
    blk = (ATT_BLOCK, hp * HEAD_DIM)
    return pl.pallas_call(
        functools.partial(_diff_attn_kernel, lam_init),
        grid=(b, N_HEADS // hp, nq),
        in_specs=[pl.BlockSpec(blk, lambda bi, h, i: (bi * nq + i, COL_DIFF_Q // hp + h)),
                  pl.BlockSpec((t, hp * HEAD_DIM), lambda bi, h, i: (bi, COL_DIFF_K // hp + h)),
                  pl.BlockSpec((t, hp * HEAD_DIM), lambda bi, h, i: (bi, COL_DIFF_V // hp + h)),
                  pl.BlockSpec(blk, lambda bi, h, i: (bi * nq + i, COL_GATE // hp + h)),
                  pl.BlockSpec((hp, 1, LANE), lambda bi, h, i: (h, 0, 0)),
                  pl.BlockSpec((4, LANE), lambda bi, h, i: (0, 0)),
                  pl.BlockSpec((1, HEAD_DIM), lambda bi, h, i: (0, 0))],
        out_specs=pl.BlockSpec(blk, lambda bi, h, i: (bi * nq + i, h)),
        out_shape=jax.ShapeDtypeStruct((m, GROUP_WIDTH), BF16),
        scratch_shapes=[_values_scratch(t)],
        compiler_params=_params("arbitrary", "arbitrary", "arbitrary"),
        name="diff_attention",
    )(front, front, front, back, slopes, lam_p, norm_w)


def _mla_attn_kernel(q_ref, k_ref, v_ref, gate_ref, o_ref, vt_ref):
    i = pl.program_id(2)
    key, qry = _block_iotas()
    mask_diag = jnp.where((key // CHUNK) <= (qry // CHUNK), 0.0, NEG_BIG)
    scale = (MLA_NOPE + MLA_ROPE) ** -0.5

    def off_scores(kb, h):
        return scale, None, 0.0

    def diag_scores(h):
        return scale, mask_diag

    _load_values_transposed(i, v_ref, vt_ref)
    dk = 2 * LANE
    qs = [q_ref[:, dk * h:dk * (h + 1)] for h in range(ATT_HEADS_PER_STEP)]
    outs = _attn_sweep(qs, k_ref, vt_ref, 0, i, off_scores, diag_scores)
    _store_gated(outs, gate_ref, o_ref)


def _store_gated(outs, gate_ref, o_ref):
    for h, (acc, l) in enumerate(outs):
        cols = slice(HEAD_DIM * h, HEAD_DIM * (h + 1))
        o_ref[:, cols] = ((acc / l).T * _silu(gate_ref[:, cols].astype(F32))).astype(o_ref.dtype)


def _mla_attention(qf, kf, vv, back, b, t):
    nq = t // ATT_BLOCK
    m = b * t
    hp = ATT_HEADS_PER_STEP
    assert (COL_GATE + 2 * N_HEADS) % hp == 0
    return pl.pallas_call(
        _mla_attn_kernel,
        grid=(b, N_HEADS // hp, nq),
        in_specs=[pl.BlockSpec((ATT_BLOCK, hp * 2 * LANE), lambda bi, h, i: (bi * nq + i, h)),
                  pl.BlockSpec((t, hp * 2 * LANE), lambda bi, h, i: (bi, h)),
                  pl.BlockSpec((t, hp * HEAD_DIM), lambda bi, h, i: (bi, h)),
                  pl.BlockSpec((ATT_BLOCK, hp * HEAD_DIM),
                               lambda bi, h, i: (bi * nq + i, (COL_GATE + 2 * N_HEADS) // hp + h))],
        out_specs=pl.BlockSpec((ATT_BLOCK, hp * HEAD_DIM), lambda bi, h, i: (bi * nq + i, h)),
        out_shape=jax.ShapeDtypeStruct((m, GROUP_WIDTH), BF16),
        scratch_shapes=[_values_scratch(t)],
        compiler_params=_params("arbitrary", "arbitrary", "arbitrary"),
        name="mla_attention",
    )(qf, kf, vv, back)


BAND_KEY_BLOCKS = (BAND_CHUNKS - 1) * CHUNK // ATT_BLOCK + 1


def _values_scratch(t):
    return pltpu.VMEM((ATT_HEADS_PER_STEP, t // ATT_BLOCK, HEAD_DIM, ATT_BLOCK), BF16)


def _band_attn_kernel(q_ref, k_ref, v_ref, gate_ref, bias_ref, o_ref, vt_ref):
    i = pl.program_id(2)
    scale = HEAD_DIM ** -0.5

    def off_scores(kb, h):
        return scale, bias_ref[h, kb - i + (BAND_KEY_BLOCKS - 1)], 0.0

    def diag_scores(h):
        return scale, bias_ref[h, BAND_KEY_BLOCKS - 1]

    lo = jnp.maximum(i - (BAND_KEY_BLOCKS - 1), 0)
    _load_values_transposed(i, v_ref, vt_ref)
    qs = [q_ref[:, HEAD_DIM * h:HEAD_DIM * (h + 1)] for h in range(ATT_HEADS_PER_STEP)]
    outs = _attn_sweep(qs, k_ref, vt_ref, lo, i, off_scores, diag_scores)
    _store_gated(outs, gate_ref, o_ref)


def _band_attention(back, bias_tiles, b, t):
    nq = t // ATT_BLOCK
    m = b * t
    hp = ATT_HEADS_PER_STEP
    assert COL_BAND_Q % hp == 0 and COL_BAND_K % hp == 0 and COL_BAND_V % hp == 0
    assert (COL_GATE + 3 * N_HEADS) % hp == 0
    blk = (ATT_BLOCK, hp * HEAD_DIM)
    return pl.pallas_call(
        _band_attn_kernel,
        grid=(b, N_HEADS // hp, nq),
        in_specs=[pl.BlockSpec(blk, lambda bi, h, i: (bi * nq + i, COL_BAND_Q // hp + h)),
                  pl.BlockSpec((t, hp * HEAD_DIM), lambda bi, h, i: (bi, COL_BAND_K // hp + h)),
                  pl.BlockSpec((t, hp * HEAD_DIM), lambda bi, h, i: (bi, COL_BAND_V // hp + h)),
                  pl.BlockSpec(blk, lambda bi, h, i: (bi * nq + i, (COL_GATE + 3 * N_HEADS) // hp + h)),
                  pl.BlockSpec((hp, BAND_KEY_BLOCKS, ATT_BLOCK, ATT_BLOCK), lambda bi, h, i: (h, 0, 0, 0))],
        out_specs=pl.BlockSpec(blk, lambda bi, h, i: (bi * nq + i, h)),
        out_shape=jax.ShapeDtypeStruct((m, GROUP_WIDTH), BF16),
        scratch_shapes=[_values_scratch(t)],
        compiler_params=_params("arbitrary", "arbitrary", "arbitrary"),
        name="band_attention",
    )(back, back, back, back, bias_tiles)


def _band_bias_tiles(rel_bias):
    n = ATT_BLOCK
    back = (BAND_KEY_BLOCKS - 1) - jnp.arange(BAND_KEY_BLOCKS)
    j = jnp.arange(2 * n)
    q_minus_k = jnp.where(j < n, j, j - 2 * n)
    rel = back[:, None] * n + q_minus_k[None, :]
    vals = rel_bias.astype(F32)[:, jnp.clip(rel, -REL_CLIP, REL_CLIP) + REL_CLIP]
    flat = jnp.tile(vals, (1, 1, n))[:, :, :n * (2 * n - 1)]
    tiles = flat.reshape(N_HEADS, BAND_KEY_BLOCKS, n, 2 * n - 1)[:, :, :, :n]
    key = jnp.arange(n)[None, :, None]
    qry = jnp.arange(n)[None, None, :]
    chunk_back = back[:, None, None] * (n // CHUNK) + qry // CHUNK - key // CHUNK
    visible = (chunk_back >= 0) & (chunk_back < BAND_CHUNKS)
    return jnp.where(visible[None], tiles, NEG_BIG)


def _rope_block(x, ct, st):
    half = MLA_ROPE // 2
    lane = lax.broadcasted_iota(jnp.int32, (1, LANE), 1)
    rot = jnp.where(lane < MLA_LANE_OFF + half, -pltpu.roll(x, LANE - half, axis=1), pltpu.roll(x, half, axis=1))
    return x * ct + rot * st


def _mla_prep_kernel(cdq_ref, ckv_ref, qg_ref, kvg_ref, w1_ref, wk_ref, wv_ref, ct_ref, st_ref,
                     q_out, k_out, v_out):
    ct = ct_ref[...]
    st = st_ref[...]
    ckv = ckv_ref[...].astype(F32)
    cq = jnp.concatenate([cdq_ref[...].astype(F32), ckv[:, :LANE]], axis=1)
    lane_q = lax.broadcasted_iota(jnp.int32, (1, cq.shape[1]), 1)
    cq = jnp.where((lane_q >= MLA_LANE_OFF) & (lane_q < MLA_LANE_OFF + MLA_Q_RANK), cq, 0.0)
    ms_q = jnp.sum(cq * cq, axis=-1, keepdims=True) * (1.0 / MLA_Q_RANK)
    nq = (cq * lax.rsqrt(ms_q + 1e-6) * qg_ref[...]).astype(BF16)
    q1 = _dot(nq, w1_ref[...])
    lane_kv = lax.broadcasted_iota(jnp.int32, (1, ckv.shape[1]), 1)
    lat = jnp.where((lane_kv >= MLA_LANE_OFF) & (lane_kv < MLA_LANE_OFF + MLA_KV_RANK), ckv, 0.0)
    ms_kv = jnp.sum(lat * lat, axis=-1, keepdims=True) * (1.0 / MLA_KV_RANK)
    nkv = (lat * lax.rsqrt(ms_kv + 1e-6) * kvg_ref[...]).astype(BF16)
    kn = _dot(nkv, wk_ref[...])
    v_out[...] = _dot(nkv, wv_ref[...]).astype(v_out.dtype)
    kr = _rope_block(ckv[:, 2 * LANE:], ct, st).astype(k_out.dtype)
    for h in range(N_HEADS):
        a, bq = 2 * LANE * h, LANE * h
        q_out[:, a:a + LANE] = q1[:, a:a + LANE].astype(q_out.dtype)
        q_out[:, a + LANE:a + 2 * LANE] = _rope_block(q1[:, a + LANE:a + 2 * LANE], ct, st).astype(q_out.dtype)
        k_out[:, a:a + LANE] = kn[:, bq:bq + LANE].astype(k_out.dtype)
        k_out[:, a + LANE:a + 2 * LANE] = kr


def _mla_prep(front, q_gain, kv_gain, w1, wk, wv, cos_t, sin_t, b, t):
    m = b * t
    tm = 256
    nt = t // tm
    full = lambda shape: pl.BlockSpec(shape, lambda i: (0, 0))
    q_blocks = MLA_Q_BLOCKS - 1
    assert COL_MLA % q_blocks == 0 and (COL_MLA + q_blocks) % MLA_KV_BLOCKS == 0
    return pl.pallas_call(
        _mla_prep_kernel,
        grid=(m // tm,),
        in_specs=[pl.BlockSpec((tm, q_blocks * LANE), lambda i: (i, COL_MLA // q_blocks)),
                  pl.BlockSpec((tm, MLA_KV_BLOCKS * LANE), lambda i: (i, (COL_MLA + q_blocks) // MLA_KV_BLOCKS)),
                  full(q_gain.shape), full(kv_gain.shape),
                  full(w1.shape), full(wk.shape), full(wv.shape),
                  pl.BlockSpec((tm, LANE), lambda i: (i % nt, 0)),
                  pl.BlockSpec((tm, LANE), lambda i: (i % nt, 0))],
        out_specs=[pl.BlockSpec((tm, 2 * GROUP_WIDTH), lambda i: (i, 0)),
                   pl.BlockSpec((tm, 2 * GROUP_WIDTH), lambda i: (i, 0)),
                   pl.BlockSpec((tm, GROUP_WIDTH), lambda i: (i, 0))],
        out_shape=[jax.ShapeDtypeStruct((m, 2 * GROUP_WIDTH), BF16),
                   jax.ShapeDtypeStruct((m, 2 * GROUP_WIDTH), BF16),
                   jax.ShapeDtypeStruct((m, GROUP_WIDTH), BF16)],
        compiler_params=_params("arbitrary"),
        name="mla_prep",
    )(front, front, q_gain, kv_gain, w1, wk, wv, cos_t, sin_t)


HALO = 16


def _gdn_prep_kernel(blocks_per_seq, x_ref, halo_ref, ab_ref, cw_ref, alog_ref, dtb_ref,
                     qn_ref, kn_ref, v_ref, gcol_ref, bcol_ref, grow_ref):
    i = pl.program_id(0)
    halo_scale = jnp.where(i % blocks_per_seq == 0, 0.0, 1.0)
    outs = (qn_ref, kn_ref, v_ref)
    for j in range(3 * N_HEADS):
        sl = slice(LANE * j, LANE * (j + 1))
        cat = jnp.concatenate([halo_ref[:, sl].astype(F32) * halo_scale, x_ref[:, sl].astype(F32)], axis=0)
        w = cw_ref[:, sl]
        acc = cat * w[GDN_CONV - 1:GDN_CONV]
        for s in range(1, GDN_CONV):
            acc = acc + pltpu.roll(cat, s, axis=0) * w[GDN_CONV - 1 - s:GDN_CONV - s]
        y = _silu(acc[HALO:])
        part, head = divmod(j, N_HEADS)
        if part < 2:
            y = y * lax.rsqrt(jnp.sum(y * y, axis=-1, keepdims=True) + 1e-6)
        if part == 0:
            y = y * HEAD_DIM ** -0.5
        outs[part][:, LANE * head:LANE * (head + 1)] = y.astype(BF16)

    ab = ab_ref[...].astype(F32)
    z = ab + dtb_ref[...]
    softplus = jnp.maximum(z, 0.0) + jnp.log1p(jnp.exp(-jnp.abs(z)))
    g = -jnp.exp(alog_ref[...]) * softplus
    beta = 1.0 / (1.0 + jnp.exp(-ab))
    row = lax.broadcasted_iota(jnp.int32, g.shape, 0) % CHUNK
    s = 1
    while s < CHUNK:
        g = g + jnp.where(row >= s, pltpu.roll(g, s, axis=0), 0.0)
        s *= 2
    gt = g.T
    for h in range(N_HEADS):
        gcol_ref[h] = jnp.broadcast_to(g[:, h:h + 1], g.shape)
        bcol_ref[h] = jnp.broadcast_to(beta[:, N_HEADS + h:N_HEADS + h + 1], g.shape)
        grow_ref[h] = gt[h:h + 1, :]


def _gdn_prep(front, conv_w, alog, dtb, b, t):
    m = b * t
    tm = GDN_GROUP
    nt = t // tm
    assert OFF_GDN_QKV % (3 * GROUP_WIDTH) == 0
    qkv_block = OFF_GDN_QKV // (3 * GROUP_WIDTH)
    return pl.pallas_call(
        functools.partial(_gdn_prep_kernel, nt),
        grid=(m // tm,),
        in_specs=[pl.BlockSpec((tm, 3 * GROUP_WIDTH), lambda i: (i, qkv_block)),
                  pl.BlockSpec((HALO, 3 * GROUP_WIDTH), lambda i: (jnp.maximum(i * (tm // HALO) - 1, 0), qkv_block)),
                  pl.BlockSpec((tm, LANE), lambda i: (i, COL_GDN_AB)),
                  pl.BlockSpec((GDN_CONV, 3 * GROUP_WIDTH), lambda i: (0, 0)),
                  pl.BlockSpec((1, LANE), lambda i: (0, 0)),
                  pl.BlockSpec((1, LANE), lambda i: (0, 0))],
        out_specs=[pl.BlockSpec((tm, GROUP_WIDTH), lambda i: (i, 0)),
                   pl.BlockSpec((tm, GROUP_WIDTH), lambda i: (i, 0)),
                   pl.BlockSpec((tm, GROUP_WIDTH), lambda i: (i, 0)),
                   pl.BlockSpec((None, N_HEADS, tm, LANE), lambda i: (i // nt, 0, i % nt, 0)),
                   pl.BlockSpec((None, N_HEADS, tm, LANE), lambda i: (i // nt, 0, i % nt, 0)),
                   pl.BlockSpec((None, N_HEADS, 1, tm), lambda i: (i // nt, 0, 0, i % nt))],
        out_shape=[jax.ShapeDtypeStruct((m, GROUP_WIDTH), BF16),
                   jax.ShapeDtypeStruct((m, GROUP_WIDTH), BF16),
                   jax.ShapeDtypeStruct((m, GROUP_WIDTH), BF16),
                   jax.ShapeDtypeStruct((b, N_HEADS, t, LANE), F32),
                   jax.ShapeDtypeStruct((b, N_HEADS, t, LANE), F32),
                   jax.ShapeDtypeStruct((b, N_HEADS, 1, t), F32)],
        compiler_params=_params("arbitrary"),
        name="gdn_prep",
    )(front, front, front, conv_w, alog, dtb)


GDN_HEADS_PER_STEP = 8


def _gdn_kernel(q_ref, k_ref, v_ref, gc_ref, bc_ref, gr_ref, gate_ref, nw_ref, o_ref, state_ref):
    n = pl.program_id(2)

    @pl.when(n == 0)
    def _():
        state_ref[...] = jnp.zeros_like(state_ref)

    g_sz = GDN_GROUP
    ri = lax.broadcasted_iota(jnp.int32, (g_sz, g_sz), 0)
    ci = lax.broadcasted_iota(jnp.int32, (g_sz, g_sz), 1)
    dif = jnp.where((ri // CHUNK) == (ci // CHUNK), ri - ci, -1)
    heads = range(GDN_HEADS_PER_STEP)
    cols = [slice(HEAD_DIM * hb, HEAD_DIM * (hb + 1)) for hb in heads]
    outs = _gdn_heads(dif, [q_ref[:, sl] for sl in cols], [k_ref[:, sl] for sl in cols],
                      [v_ref[:, sl] for sl in cols], [gc_ref[hb] for hb in heads], [bc_ref[hb] for hb in heads],
                      [gr_ref[hb] for hb in heads], state_ref)
    for hb in heads:
        o = outs[hb]
        o = o * lax.rsqrt(jnp.mean(o * o, axis=-1, keepdims=True) + 1e-6) * nw_ref[...]
        o_ref[:, cols[hb]] = (o * _silu(gate_ref[:, cols[hb]].astype(F32))).astype(o_ref.dtype)


def _gdn_heads(dif, q16, k16, v16, gc, beta, gr, state_ref):
    heads = range(len(q16))
    n_chunks = GDN_GROUP // CHUNK
    incl = dif >= 0
    kf = [k16[h].astype(F32) for h in heads]
    eg = [jnp.exp(gc[h]) for h in heads]
    kbeta = [kf[h] * beta[h] for h in heads]
    decay = [jnp.where(incl, jnp.exp(jnp.where(incl, jnp.concatenate([gc[h], gc[h]], axis=1) - gr[h], 0.0)), 0.0)
             for h in heads]

    kk = [_dot_nt(kbeta[h].astype(BF16), k16[h]) for h in heads]
    mpow = [jnp.where(dif > 0, -(kk[h] * decay[h]), 0.0) for h in heads]
    inv = [jnp.where(dif == 0, 1.0, 0.0) + mpow[h] for h in heads]
    mpow16 = [mpow[h].astype(BF16) for h in heads]
    for _ in range(int(math.log2(CHUNK)) - 1):
        mpow16 = [_dot(mpow16[h], mpow16[h]).astype(BF16) for h in heads]
        inv = [inv[h] + _dot(inv[h].astype(BF16), mpow16[h]) for h in heads]

    rhs = [jnp.concatenate([v16[h].astype(F32) * beta[h], kbeta[h] * eg[h]], axis=1).astype(BF16) for h in heads]
    uw = [_dot(inv[h].astype(BF16), rhs[h]) for h in heads]
    u = [uw[h][:, :HEAD_DIM] for h in heads]
    w = [uw[h][:, HEAD_DIM:].astype(BF16) for h in heads]
    qk = [(_dot_nt(q16[h], k16[h]) * decay[h]).astype(BF16) for h in heads]
    qe = [(q16[h].astype(F32) * eg[h]).astype(BF16) for h in heads]
    g_last_rows = [[gc[h][CHUNK * (c + 1) - 1:CHUNK * (c + 1), :] for c in range(n_chunks)] for h in heads]
    kdec_t = [(kf[h] * jnp.exp(jnp.concatenate([jnp.broadcast_to(g, (CHUNK, LANE)) for g in g_last_rows[h]], axis=0)
                               - gc[h])).T.astype(BF16) for h in heads]
    col_chunk = lax.broadcasted_iota(jnp.int32, kdec_t[0].shape, 1) // CHUNK

    vparts = [[u[h][CHUNK * c:CHUNK * (c + 1)] for c in range(n_chunks)] for h in heads]
    state = [state_ref[h] for h in heads]
    outs = [[] for _ in heads]
    for c in range(n_chunks):
        rows = slice(CHUNK * c, CHUNK * (c + 1))
        s16 = [state[h].astype(BF16) for h in heads]
        ws = [_dot(w[h][rows], s16[h]) for h in heads]
        for h in heads:
            vparts[h][c] = u[h][rows] - ws[h]
        vfull = [jnp.concatenate(vparts[h], axis=0).astype(BF16) for h in heads]
        upd = [_dot(jnp.where(col_chunk == c, kdec_t[h], jnp.zeros_like(kdec_t[h])), vfull[h]) for h in heads]
        state = [state[h] * jnp.exp(g_last_rows[h][c]) + upd[h] for h in heads]
        for h in heads:
            outs[h].append(_dot(qe[h][rows], s16[h]) + _dot(qk[h][rows], vfull[h]))
    for h in heads:
        state_ref[h] = state[h]
    return [jnp.concatenate(outs[h], axis=0) for h in heads]


def _gated_delta_net(qn, kn, vv, gcol, bcol, grow, back, norm_w, b, t):
    assert (COL_GATE + N_HEADS) % GDN_HEADS_PER_STEP == 0
    m = b * t
    ng = t // GDN_GROUP
    hb = GDN_HEADS_PER_STEP
    blk = (GDN_GROUP, HEAD_DIM * hb)
    tok = lambda bi, h, n: (bi * ng + n, h)
    return pl.pallas_call(
        _gdn_kernel,
        grid=(b, N_HEADS // hb, ng),
        in_specs=[pl.BlockSpec(blk, tok), pl.BlockSpec(blk, tok), pl.BlockSpec(blk, tok),
                  pl.BlockSpec((None, hb, GDN_GROUP, LANE), lambda bi, h, n: (bi, h, n, 0)),
                  pl.BlockSpec((None, hb, GDN_GROUP, LANE), lambda bi, h, n: (bi, h, n, 0)),
                  pl.BlockSpec((None, hb, 1, GDN_GROUP), lambda bi, h, n: (bi, h, 0, n)),
                  pl.BlockSpec(blk, lambda bi, h, n: (bi * ng + n, (COL_GATE + N_HEADS) // hb + h)),
                  pl.BlockSpec((1, HEAD_DIM), lambda bi, h, n: (0, 0))],
        out_specs=pl.BlockSpec(blk, tok),
        out_shape=jax.ShapeDtypeStruct((m, GROUP_WIDTH), BF16),
        scratch_shapes=[pltpu.VMEM((hb, HEAD_DIM, HEAD_DIM), F32)],
        compiler_params=_params("arbitrary", "arbitrary", "arbitrary"),
        name="gated_delta_net",
    )(qn, kn, vv, gcol, bcol, grow, back, norm_w)


def _out_proj_kernel(oa_ref, ob_ref, oc_ref, od_ref, w_ref, x_ref, g_ref, b_ref, y_ref, *y16_ref):
    o = jnp.concatenate([oa_ref[...], ob_ref[...], oc_ref[...], od_ref[...]], axis=1)
    z = _dot(o, w_ref[...]) + DEEPNORM_ALPHA * x_ref[...]
    zc = z - jnp.mean(z, axis=-1, keepdims=True)
    y = zc * lax.rsqrt(jnp.mean(zc * zc, axis=-1, keepdims=True) + 1e-5) * g_ref[...] + b_ref[...]
    y_ref[...] = y
    for ref in y16_ref:
        ref[...] = y.astype(BF16)


def _out_proj_ln(oa, ob, oc, od, w16, x2d, gain, bias, with_bf16):
    m = x2d.shape[0]
    tm = 128
    grp = pl.BlockSpec((tm, GROUP_WIDTH), lambda i: (i, 0))
    row = pl.BlockSpec((tm, D_MODEL), lambda i: (i, 0))
    vec = pl.BlockSpec((1, D_MODEL), lambda i: (0, 0))
    weight = pl.BlockSpec((D_MODEL, D_MODEL), lambda i: (0, 0), pipeline_mode=pl.Buffered(1))
    return pl.pallas_call(
        _out_proj_kernel,
        grid=(m // tm,),
        in_specs=[grp, grp, grp, grp, weight, row, vec, vec],
        out_specs=[row, row][:1 + with_bf16],
        out_shape=[jax.ShapeDtypeStruct((m, D_MODEL), F32),
                   jax.ShapeDtypeStruct((m, D_MODEL), BF16)][:1 + with_bf16],
        compiler_params=_params("arbitrary"),
        name="out_proj_layernorm",
    )(oa, ob, oc, od, w16, x2d, gain, bias)


def _place(a, axis, offset, size):
    pad = [(0, 0)] * a.ndim
    pad[axis] = (offset, size - offset - a.shape[axis])
    return jnp.pad(a, pad)


def _layout_w_uq(w):
    w = w.reshape(MLA_Q_RANK, N_HEADS, MLA_NOPE + MLA_ROPE)
    w1 = jnp.concatenate([w[:, :, :MLA_NOPE], _place(w[:, :, MLA_NOPE:], 2, MLA_LANE_OFF, LANE)], axis=2)
    w1 = w1.reshape(MLA_Q_RANK, N_HEADS * 2 * LANE)
    return _place(w1, 0, MLA_LANE_OFF, MLA_Q_BLOCKS * LANE).astype(BF16)


def _layout_w_ukv(w):
    w = w.reshape(MLA_KV_RANK, N_HEADS, MLA_NOPE + HEAD_DIM)
    wk = w[:, :, :MLA_NOPE].reshape(MLA_KV_RANK, GROUP_WIDTH)
    wv = w[:, :, MLA_NOPE:].reshape(MLA_KV_RANK, GROUP_WIDTH)
    rows = MLA_KV_BLOCKS * LANE
    return _place(wk, 0, MLA_LANE_OFF, rows).astype(BF16), _place(wv, 0, MLA_LANE_OFF, rows).astype(BF16)


def _rope_tables(t):
    half = MLA_ROPE // 2
    inv = ROPE_THETA ** (-jnp.arange(half, dtype=F32) / half)
    ang = jnp.arange(t).astype(F32)[:, None] * inv[None, :]
    cos_t = _place(jnp.concatenate([jnp.cos(ang), jnp.cos(ang)], axis=1), 1, MLA_LANE_OFF, LANE)
    sin_t = _place(jnp.concatenate([jnp.sin(ang), jnp.sin(ang)], axis=1), 1, MLA_LANE_OFF, LANE)
    return cos_t, sin_t


def _pad_lanes(v):
    return jnp.concatenate([v.astype(F32), jnp.zeros((LANE - v.shape[0],), F32)])[None, :]


def _layer(x2d, x16, layer_idx, b, t, w_in, diff_lambda, diff_norm, gdn_conv, gdn_a_log, gdn_dt_bias, gdn_norm,
           mla_q_norm, mla_w_uq, mla_kv_norm, mla_w_ukv, rel_bias, w_out, ln_gain, ln_bias, tables):
    front, back = _in_proj(x16, w_in, layer_idx)

    lam_init = 0.8 - 0.6 * math.exp(-0.3 * layer_idx)
    slopes = 2.0 ** (-8.0 * jnp.arange(1, N_HEADS + 1, dtype=F32) / N_HEADS)
    slopes = jnp.broadcast_to(slopes[:, None, None], (N_HEADS, 1, LANE))
    lam_p = jnp.concatenate([diff_lambda.astype(F32), jnp.zeros((4, LANE - DIFF_HALF), F32)], axis=1)
    o_a = _diff_attention(front, back, slopes, lam_p, diff_norm.astype(F32)[None, :], lam_init, b, t)

    qn, kn, vv, gcol, bcol, grow = _gdn_prep(front, gdn_conv.astype(F32), _pad_lanes(gdn_a_log),
                                              _pad_lanes(gdn_dt_bias), b, t)
    o_b = _gated_delta_net(qn, kn, vv, gcol, bcol, grow, back, gdn_norm.astype(F32)[None, :], b, t)

    w1 = _layout_w_uq(mla_w_uq)
    wk, wv = _layout_w_ukv(mla_w_ukv)
    cos_t, sin_t = tables
    q_gain = _place(mla_q_norm.astype(F32)[None, :], 1, MLA_LANE_OFF, MLA_Q_BLOCKS * LANE)
    kv_gain = _place(mla_kv_norm.astype(F32)[None, :], 1, MLA_LANE_OFF, MLA_KV_BLOCKS * LANE)
    qf, kf, vc = _mla_prep(front, q_gain, kv_gain, w1, wk, wv, cos_t, sin_t, b, t)
    o_c = _mla_attention(qf, kf, vc, back, b, t)

    o_d = _band_attention(back, _band_bias_tiles(rel_bias), b, t)

    outs = _out_proj_ln(o_a, o_b, o_c, o_d, w_out.astype(BF16), x2d,
                        ln_gain.astype(F32)[None, :], ln_bias.astype(F32)[None, :], layer_idx + 1 < DEPTH)
    return outs[0], (outs[1] if len(outs) > 1 else None)


def kernel(x, w_in, diff_lambda, diff_norm, gdn_conv, gdn_a_log, gdn_dt_bias, gdn_norm, mla_q_norm, mla_w_uq,
           mla_kv_norm, mla_w_ukv, rel_bias, w_out, ln_gain, ln_bias):
    b, t, d = x.shape
    assert d == D_MODEL and t % GDN_GROUP == 0 and (b * t) % 512 == 0
    tables = _rope_tables(t)
    x2d = x.reshape(b * t, d)
    x16 = x2d.astype(BF16)
    w_t = jnp.swapaxes(w_in, 1, 2)
    for l in range(DEPTH):
        x2d, x16 = _layer(x2d, x16, l, b, t, w_t, diff_lambda[l], diff_norm[l], gdn_conv[l], gdn_a_log[l],
                          gdn_dt_bias[l], gdn_norm[l], mla_q_norm[l], mla_w_uq[l], mla_kv_norm[l],
                          mla_w_ukv[l], rel_bias[l], w_out[l], ln_gain[l], ln_bias[l], tables)
    return x2d.reshape(b, t, d)
```

```python
import functools
import math

import jax
import jax.numpy as jnp
from jax import lax
from jax.experimental import pallas as pl
from jax.experimental.pallas import tpu as pltpu

F32 = jnp.float32
BF16 = jnp.bfloat16

D_MODEL = 4096
DEPTH = 2
CHUNK = 64
N_HEADS = 8
HEAD_DIM = 128
GROUP_WIDTH = N_HEADS * HEAD_DIM
DIFF_HALF = HEAD_DIM // 2
GDN_CONV = 4
MLA_Q_RANK = 768
MLA_KV_RANK = 256
MLA_NOPE = 128
MLA_ROPE = 64
ROPE_THETA = 10000.0
BAND_CHUNKS = 9
REL_CLIP = 128
DEEPNORM_ALPHA = (2 * DEPTH) ** 0.25
IN_SIZES = (GROUP_WIDTH, GROUP_WIDTH, GROUP_WIDTH, 3 * GROUP_WIDTH, N_HEADS, N_HEADS,
            MLA_Q_RANK, MLA_KV_RANK + MLA_ROPE, GROUP_WIDTH, GROUP_WIDTH, GROUP_WIDTH, D_MODEL)

LANE = 128
IN_COLS = sum(IN_SIZES)
(OFF_DIFF_Q, OFF_DIFF_K, OFF_DIFF_V, OFF_GDN_QKV, OFF_GDN_A, OFF_GDN_B, OFF_MLA_Q, OFF_MLA_KV,
 OFF_BAND_Q, OFF_BAND_K, OFF_BAND_V, OFF_GATE) = (sum(IN_SIZES[:n]) for n in range(len(IN_SIZES)))

FRONT_TILE = 768
FRONT_COLS = 10 * FRONT_TILE
BACK_TILE = 512
BACK_SHIFT = OFF_BAND_Q % LANE
BACK_COLS = IN_COLS - OFF_BAND_Q
NEXT_ROWS = LANE
MLA_LANE_OFF = OFF_MLA_Q % LANE
assert OFF_GDN_QKV % LANE == 0 and OFF_GDN_A % LANE == 0 and OFF_GDN_B == OFF_GDN_A + N_HEADS
assert OFF_MLA_KV % LANE == MLA_LANE_OFF and OFF_GATE % LANE == BACK_SHIFT and BACK_COLS % BACK_TILE == 0
assert FRONT_COLS >= OFF_BAND_Q and (OFF_BAND_Q - BACK_SHIFT) % BACK_TILE == 0
COL_DIFF_Q = OFF_DIFF_Q // LANE
COL_DIFF_K = OFF_DIFF_K // LANE
COL_DIFF_V = OFF_DIFF_V // LANE
COL_GDN_QKV = OFF_GDN_QKV // LANE
COL_GDN_AB = OFF_GDN_A // LANE
COL_MLA = OFF_MLA_Q // LANE
COL_BAND_Q = 0
COL_BAND_K = (OFF_BAND_K - OFF_BAND_Q) // LANE
COL_BAND_V = (OFF_BAND_V - OFF_BAND_Q) // LANE
COL_GATE = (OFF_GATE - OFF_BAND_Q) // LANE
MLA_Q_BLOCKS = MLA_Q_RANK // LANE + 1
MLA_KV_BLOCKS = 3

ATT_BLOCK = 256
GDN_GROUP = 256
NEG_BIG = -1e30
VMEM_LIMIT = 48 * 1024 * 1024
IN_PROJ_ROWS = 512
BACK_VMEM_LIMIT = (2 * (BACK_TILE + NEXT_ROWS) * D_MODEL * 4 + BACK_TILE * D_MODEL * 2
                   + 2 * IN_PROJ_ROWS * D_MODEL * 2 + IN_PROJ_ROWS * BACK_TILE * (2 * 2 + 4) + (4 << 20))


def _params(*sem, vmem=VMEM_LIMIT):
    return pltpu.CompilerParams(dimension_semantics=sem, vmem_limit_bytes=vmem)


def _dot(a, b):
    return jnp.dot(a, b, preferred_element_type=F32)


def _dot_nt(a, b):
    return lax.dot_general(a, b, (((1,), (1,)), ((), ())), preferred_element_type=F32)


def _silu(x):
    return x * (1.0 / (1.0 + jnp.exp(-x)))


CAST_ROWS = 16


def _cast_rows(src_ref, src_row, dst_ref, dst_row, n_rows):
    def body(r, carry):
        src = pl.ds(pl.multiple_of(src_row + r * CAST_ROWS, CAST_ROWS), CAST_ROWS)
        dst = pl.ds(pl.multiple_of(dst_row + r * CAST_ROWS, CAST_ROWS), CAST_ROWS)
        dst_ref[dst, :] = src_ref[src, :].astype(BF16)
        return carry
    lax.fori_loop(0, n_rows // CAST_ROWS, body, 0)


def _in_proj_front_kernel(x_ref, w_ref, o_ref, w16_ref):
    @pl.when(pl.program_id(1) == 0)
    def _():
        _cast_rows(w_ref, 0, w16_ref, 0, w_ref.shape[0])

    o_ref[...] = _dot_nt(x_ref[...], w16_ref[...]).astype(o_ref.dtype)


def _in_proj_back_kernel(x_ref, w_ref, w_next_ref, o_ref, w16_ref):
    @pl.when(pl.program_id(1) == 0)
    def _():
        own = w_ref.shape[0] - BACK_SHIFT
        _cast_rows(w_ref, BACK_SHIFT, w16_ref, 0, own)
        _cast_rows(w_next_ref, 0, w16_ref, own, BACK_SHIFT)

    o_ref[...] = _dot_nt(x_ref[...], w16_ref[...]).astype(o_ref.dtype)


def _in_proj(xb, w_t, layer):
    m = xb.shape[0]
    tm = IN_PROJ_ROWS
    x_spec = pl.BlockSpec((tm, D_MODEL), lambda j, i: (i, 0))
    front = pl.pallas_call(
        _in_proj_front_kernel,
        grid=(FRONT_COLS // FRONT_TILE, m // tm),
        in_specs=[x_spec, pl.BlockSpec((None, FRONT_TILE, D_MODEL), lambda j, i: (layer, j, 0))],
        out_specs=pl.BlockSpec((tm, FRONT_TILE), lambda j, i: (i, j)),
        out_shape=jax.ShapeDtypeStruct((m, FRONT_COLS), BF16),
        scratch_shapes=[pltpu.VMEM((FRONT_TILE, D_MODEL), BF16)],
        compiler_params=_params("arbitrary", "arbitrary"),
        name="in_proj_front",
    )(xb, w_t)
    first_tile = (OFF_BAND_Q - BACK_SHIFT) // BACK_TILE
    next_blocks = BACK_TILE // NEXT_ROWS
    back = pl.pallas_call(
        _in_proj_back_kernel,
        grid=(BACK_COLS // BACK_TILE, m // tm),
        in_specs=[x_spec,
                  pl.BlockSpec((None, BACK_TILE, D_MODEL), lambda j, i: (layer, first_tile + j, 0)),
                  pl.BlockSpec((None, NEXT_ROWS, D_MODEL), lambda j, i: (layer, (first_tile + j + 1) * next_blocks, 0))],
        out_specs=pl.BlockSpec((tm, BACK_TILE), lambda j, i: (i, j)),
        out_shape=jax.ShapeDtypeStruct((m, BACK_COLS), BF16),
        scratch_shapes=[pltpu.VMEM((BACK_TILE, D_MODEL), BF16)],
        compiler_params=_params("arbitrary", "arbitrary", vmem=BACK_VMEM_LIMIT),
        name="in_proj_back",
    )(xb, w_t, w_t)
    return front, back


def _softmax_stats(m, l, s, scale, bias, shift):
    t = s if scale == 1.0 else s * scale
    if bias is not None:
        t = t + bias
    m_new = jnp.maximum(m, jnp.max(t, axis=0, keepdims=True) + shift)
    alpha = jnp.exp(m - m_new)
    p = jnp.exp(t - (m_new - shift))
    return m_new, alpha * l + jnp.sum(p, axis=0, keepdims=True), alpha, p.astype(BF16)


DIFF_HEADS_PER_STEP = 4
MLA_HEADS_PER_STEP = 8
BAND_HEADS_PER_STEP = 8


def _load_values_transposed(step, v_ref, vt_ref):
    @pl.when(step == 0)
    def _():
        for h in range(vt_ref.shape[0]):
            for j in range(vt_ref.shape[1]):
                v = v_ref[ATT_BLOCK * j:ATT_BLOCK * (j + 1), HEAD_DIM * h:HEAD_DIM * (h + 1)]
                vt_ref[h, j] = v.astype(F32).T.astype(BF16)


def _attn_sweep(qs, k_ref, vt_ref, lo, i, off_scores, diag_scores):
    heads = range(len(qs))
    nq, dk = qs[0].shape

    def keys(kb, h):
        return k_ref[pl.ds(pl.multiple_of(kb * ATT_BLOCK, ATT_BLOCK), ATT_BLOCK), dk * h:dk * (h + 1)]

    def body(kb, carry):
        s_next = [_dot_nt(keys(kb + 1, h), qs[h]) for h in heads]
        prev = jnp.maximum(kb - 1, lo)
        acc = [carry[h][5] * carry[h][2] + _dot(vt_ref[h, prev], carry[h][4]) for h in heads]
        stats = [_softmax_stats(carry[h][0], carry[h][1], carry[h][3], *off_scores(kb, h)) for h in heads]
        return tuple((stats[h][0], stats[h][1], acc[h], s_next[h], stats[h][3], stats[h][2]) for h in heads)

    init = tuple((jnp.full((1, nq), NEG_BIG, F32), jnp.zeros((1, nq), F32), jnp.zeros((HEAD_DIM, nq), F32),
                  _dot_nt(keys(lo, h), qs[h]), jnp.zeros((ATT_BLOCK, nq), BF16), jnp.ones((1, nq), F32))
                 for h in heads)
    carry = lax.fori_loop(lo, i, body, init)
    prev = jnp.maximum(i - 1, lo)
    acc = [carry[h][5] * carry[h][2] + _dot(vt_ref[h, prev], carry[h][4]) for h in heads]
    stats = [_softmax_stats(carry[h][0], carry[h][1], carry[h][3], *diag_scores(h), 0.0) for h in heads]
    return [(stats[h][2] * acc[h] + _dot(vt_ref[h, i], stats[h][3]), stats[h][1]) for h in heads]


def _block_iotas():
    key = lax.broadcasted_iota(jnp.int32, (ATT_BLOCK, ATT_BLOCK), 0)
    qry = lax.broadcasted_iota(jnp.int32, (ATT_BLOCK, ATT_BLOCK), 1)
    return key, qry


def _diff_attn_kernel(lam_init, q_ref, k_ref, v_ref, gate_ref, slope_ref, lam_ref, nw_ref, o_ref, vt_ref):
    i = pl.program_id(2)
    heads = range(DIFF_HEADS_PER_STEP)
    cols = [slice(HEAD_DIM * h, HEAD_DIM * (h + 1)) for h in heads]
    key, qry = _block_iotas()
    dist = (qry - key).astype(F32)
    visible = (key // CHUNK) <= (qry // CHUNK)
    qs, slopes, bias_off, bias_diag = [], [], [], []
    for h in heads:
        q = q_ref[:, cols[h]] * (DIFF_HALF ** -0.5)
        lane = lax.broadcasted_iota(jnp.int32, q.shape, 1)
        qs.append(jnp.concatenate([jnp.where(lane < DIFF_HALF, q, jnp.zeros_like(q)),
                                   jnp.where(lane >= DIFF_HALF, q, jnp.zeros_like(q))], axis=0))
        slope = slope_ref[h][:, 0:1]
        slopes.append(slope)
        off = -slope * dist
        diag = jnp.where(visible, -slope * jnp.abs(dist), NEG_BIG)
        bias_off.append(jnp.concatenate([off, off], axis=1))
        bias_diag.append(jnp.concatenate([diag, diag], axis=1))

    def off_scores(kb, h):
        return 1.0, bias_off[h], -slopes[h] * ((i - kb) * ATT_BLOCK).astype(F32)

    def diag_scores(h):
        return 1.0, bias_diag[h]

    _load_values_transposed(i, v_ref, vt_ref)
    outs = _attn_sweep(qs, k_ref, vt_ref, 0, i, off_scores, diag_scores)
    lp = lam_ref[...]
    lam = (jnp.exp(jnp.sum(lp[0:1] * lp[1:2], axis=-1, keepdims=True))
           - jnp.exp(jnp.sum(lp[2:3] * lp[3:4], axis=-1, keepdims=True)) + lam_init)
    for h in heads:
        acc, l = outs[h]
        o = acc / l
        o = (o[:, :ATT_BLOCK] - lam * o[:, ATT_BLOCK:]).T
        o = o * lax.rsqrt(jnp.mean(o * o, axis=-1, keepdims=True) + 1e-6) * nw_ref[...]
        o = o * (1.0 - lam_init)
        o_ref[:, cols[h]] = (o * _silu(gate_ref[:, cols[h]].astype(F32))).astype(o_ref.dtype)


def _diff_attention(front, back, slopes, lam_p, norm_w, lam_init, b, t):
    nq = t // ATT_BLOCK
    m = b * t
    hp = DIFF_HEADS_PER_STEP
    assert COL_DIFF_Q % hp == 0 and COL_DIFF_K % hp == 0 and COL_DIFF_V % hp == 0 and COL_GATE % hp == 0
    blk = (ATT_BLOCK, hp * HEAD_DIM)
    return pl.pallas_call(
        functools.partial(_diff_attn_kernel, lam_init),
        grid=(b, N_HEADS // hp, nq),
        in_specs=[pl.BlockSpec(blk, lambda bi, h, i: (bi * nq + i, COL_DIFF_Q // hp + h)),
                  pl.BlockSpec((t, hp * HEAD_DIM), lambda bi, h, i: (bi, COL_DIFF_K // hp + h)),
                  pl.BlockSpec((t, hp * HEAD_DIM), lambda bi, h, i: (bi, COL_DIFF_V // hp + h)),
                  pl.BlockSpec(blk, lambda bi, h, i: (bi * nq + i, COL_GATE // hp + h)),
                  pl.BlockSpec((hp, 1, LANE), lambda bi, h, i: (h, 0, 0)),
                  pl.BlockSpec((4, LANE), lambda bi, h, i: (0, 0)),
                  pl.BlockSpec((1, HEAD_DIM), lambda bi, h, i: (0, 0))],
        out_specs=pl.BlockSpec(blk, lambda bi, h, i: (bi * nq + i, h)),
        out_shape=jax.ShapeDtypeStruct((m, GROUP_WIDTH), BF16),
        scratch_shapes=[_values_scratch(t, hp)],
        compiler_params=_params("arbitrary", "arbitrary", "arbitrary"),
        name="diff_attention",
    )(front, front, front, back, slopes, lam_p, norm_w)


def _mla_attn_kernel(q_ref, k_ref, v_ref, gate_ref, o_ref, vt_ref):
    i = pl.program_id(2)
    key, qry = _block_iotas()
    mask_diag = jnp.where((key // CHUNK) <= (qry // CHUNK), 0.0, NEG_BIG)
    scale = (MLA_NOPE + MLA_ROPE) ** -0.5

    def off_scores(kb, h):
        return scale, None, 0.0

    def diag_scores(h):
        return scale, mask_diag

    _load_values_transposed(i, v_ref, vt_ref)
    dk = 2 * LANE
    qs = [q_ref[:, dk * h:dk * (h + 1)] for h in range(MLA_HEADS_PER_STEP)]
    outs = _attn_sweep(qs, k_ref, vt_ref, 0, i, off_scores, diag_scores)
    _store_gated(outs, gate_ref, o_ref)


def _store_gated(outs, gate_ref, o_ref):
    for h, (acc, l) in enumerate(outs):
        cols = slice(HEAD_DIM * h, HEAD_DIM * (h + 1))
        o_ref[:, cols] = ((acc / l).T * _silu(gate_ref[:, cols].astype(F32))).astype(o_ref.dtype)


def _mla_attention(qf, kf, vv, back, b, t):
    nq = t // ATT_BLOCK
    m = b * t
    hp = MLA_HEADS_PER_STEP
    assert (COL_GATE + 2 * N_HEADS) % hp == 0
    return pl.pallas_call(
        _mla_attn_kernel,
        grid=(b, N_HEADS // hp, nq),
        in_specs=[pl.BlockSpec((ATT_BLOCK, hp * 2 * LANE), lambda bi, h, i: (bi * nq + i, h)),
                  pl.BlockSpec((t, hp * 2 * LANE), lambda bi, h, i: (bi, h)),
                  pl.BlockSpec((t, hp * HEAD_DIM), lambda bi, h, i: (bi, h)),
                  pl.BlockSpec((ATT_BLOCK, hp * HEAD_DIM),
                               lambda bi, h, i: (bi * nq + i, (COL_GATE + 2 * N_HEADS) // hp + h))],
        out_specs=pl.BlockSpec((ATT_BLOCK, hp * HEAD_DIM), lambda bi, h, i: (bi * nq + i, h)),
        out_shape=jax.ShapeDtypeStruct((m, GROUP_WIDTH), BF16),
        scratch_shapes=[_values_scratch(t, hp)],
        compiler_params=_params("arbitrary", "arbitrary", "arbitrary"),
        name="mla_attention",
    )(qf, kf, vv, back)


BAND_KEY_BLOCKS = (BAND_CHUNKS - 1) * CHUNK // ATT_BLOCK + 1


def _values_scratch(t, heads):
    return pltpu.VMEM((heads, t // ATT_BLOCK, HEAD_DIM, ATT_BLOCK), BF16)


def _band_attn_kernel(q_ref, k_ref, v_ref, gate_ref, bias_ref, o_ref, vt_ref):
    i = pl.program_id(2)
    scale = HEAD_DIM ** -0.5

    def off_scores(kb, h):
        return scale, bias_ref[h, kb - i + (BAND_KEY_BLOCKS - 1)], 0.0

    def diag_scores(h):
        return scale, bias_ref[h, BAND_KEY_BLOCKS - 1]

    lo = jnp.maximum(i - (BAND_KEY_BLOCKS - 1), 0)
    _load_values_transposed(i, v_ref, vt_ref)
    qs = [q_ref[:, HEAD_DIM * h:HEAD_DIM * (h + 1)] for h in range(BAND_HEADS_PER_STEP)]
    outs = _attn_sweep(qs, k_ref, vt_ref, lo, i, off_scores, diag_scores)
    _store_gated(outs, gate_ref, o_ref)


def _band_attention(back, bias_tiles, b, t):
    nq = t // ATT_BLOCK
    m = b * t
    hp = BAND_HEADS_PER_STEP
    assert COL_BAND_Q % hp == 0 and COL_BAND_K % hp == 0 and COL_BAND_V % hp == 0
    assert (COL_GATE + 3 * N_HEADS) % hp == 0
    blk = (ATT_BLOCK, hp * HEAD_DIM)
    return pl.pallas_call(
        _band_attn_kernel,
        grid=(b, N_HEADS // hp, nq),
        in_specs=[pl.BlockSpec(blk, lambda bi, h, i: (bi * nq + i, COL_BAND_Q // hp + h)),
                  pl.BlockSpec((t, hp * HEAD_DIM), lambda bi, h, i: (bi, COL_BAND_K // hp + h)),
                  pl.BlockSpec((t, hp * HEAD_DIM), lambda bi, h, i: (bi, COL_BAND_V // hp + h)),
                  pl.BlockSpec(blk, lambda bi, h, i: (bi * nq + i, (COL_GATE + 3 * N_HEADS) // hp + h)),
                  pl.BlockSpec((hp, BAND_KEY_BLOCKS, ATT_BLOCK, ATT_BLOCK), lambda bi, h, i: (h, 0, 0, 0))],
        out_specs=pl.BlockSpec(blk, lambda bi, h, i: (bi * nq + i, h)),
        out_shape=jax.ShapeDtypeStruct((m, GROUP_WIDTH), BF16),
        scratch_shapes=[_values_scratch(t, hp)],
        compiler_params=_params("arbitrary", "arbitrary", "arbitrary"),
        name="band_attention",
    )(back, back, back, back, bias_tiles)


def _band_bias_tiles(rel_bias):
    n = ATT_BLOCK
    back = (BAND_KEY_BLOCKS - 1) - jnp.arange(BAND_KEY_BLOCKS)
    j = jnp.arange(2 * n)
    q_minus_k = jnp.where(j < n, j, j - 2 * n)
    rel = back[:, None] * n + q_minus_k[None, :]
    vals = rel_bias.astype(F32)[:, jnp.clip(rel, -REL_CLIP, REL_CLIP) + REL_CLIP]
    flat = jnp.tile(vals, (1, 1, n))[:, :, :n * (2 * n - 1)]
    tiles = flat.reshape(N_HEADS, BAND_KEY_BLOCKS, n, 2 * n - 1)[:, :, :, :n]
    key = jnp.arange(n)[None, :, None]
    qry = jnp.arange(n)[None, None, :]
    chunk_back = back[:, None, None] * (n // CHUNK) + qry // CHUNK - key // CHUNK
    visible = (chunk_back >= 0) & (chunk_back < BAND_CHUNKS)
    return jnp.where(visible[None], tiles, NEG_BIG)


def _rope_block(x, ct, st):
    half = MLA_ROPE // 2
    lane = lax.broadcasted_iota(jnp.int32, (1, LANE), 1)
    rot = jnp.where(lane < MLA_LANE_OFF + half, -pltpu.roll(x, LANE - half, axis=1), pltpu.roll(x, half, axis=1))
    return x * ct + rot * st


def _mla_prep_kernel(cdq_ref, ckv_ref, qg_ref, kvg_ref, w1_ref, wk_ref, wv_ref, ct_ref, st_ref,
                     q_out, k_out, v_out):
    ct = ct_ref[...]
    st = st_ref[...]
    ckv = ckv_ref[...].astype(F32)
    cq = jnp.concatenate([cdq_ref[...].astype(F32), ckv[:, :LANE]], axis=1)
    lane_q = lax.broadcasted_iota(jnp.int32, (1, cq.shape[1]), 1)
    cq = jnp.where((lane_q >= MLA_LANE_OFF) & (lane_q < MLA_LANE_OFF + MLA_Q_RANK), cq, 0.0)
    ms_q = jnp.sum(cq * cq, axis=-1, keepdims=True) * (1.0 / MLA_Q_RANK)
    nq = (cq * lax.rsqrt(ms_q + 1e-6) * qg_ref[...]).astype(BF16)
    q1 = _dot(nq, w1_ref[...])
    lane_kv = lax.broadcasted_iota(jnp.int32, (1, ckv.shape[1]), 1)
    lat = jnp.where((lane_kv >= MLA_LANE_OFF) & (lane_kv < MLA_LANE_OFF + MLA_KV_RANK), ckv, 0.0)
    ms_kv = jnp.sum(lat * lat, axis=-1, keepdims=True) * (1.0 / MLA_KV_RANK)
    nkv = (lat * lax.rsqrt(ms_kv + 1e-6) * kvg_ref[...]).astype(BF16)
    kn = _dot(nkv, wk_ref[...])
    v_out[...] = _dot(nkv, wv_ref[...]).astype(v_out.dtype)
    kr = _rope_block(ckv[:, 2 * LANE:], ct, st).astype(k_out.dtype)
    for h in range(N_HEADS):
        a, bq = 2 * LANE * h, LANE * h
        q_out[:, a:a + LANE] = q1[:, a:a + LANE].astype(q_out.dtype)
        q_out[:, a + LANE:a + 2 * LANE] = _rope_block(q1[:, a + LANE:a + 2 * LANE], ct, st).astype(q_out.dtype)
        k_out[:, a:a + LANE] = kn[:, bq:bq + LANE].astype(k_out.dtype)
        k_out[:, a + LANE:a + 2 * LANE] = kr


def _mla_prep(front, q_gain, kv_gain, w1, wk, wv, cos_t, sin_t, b, t):
    m = b * t
    tm = 256
    nt = t // tm
    full = lambda shape: pl.BlockSpec(shape, lambda i: (0, 0))
    q_blocks = MLA_Q_BLOCKS - 1
    assert COL_MLA % q_blocks == 0 and (COL_MLA + q_blocks) % MLA_KV_BLOCKS == 0
    return pl.pallas_call(
        _mla_prep_kernel,
        grid=(m // tm,),
        in_specs=[pl.BlockSpec((tm, q_blocks * LANE), lambda i: (i, COL_MLA // q_blocks)),
                  pl.BlockSpec((tm, MLA_KV_BLOCKS * LANE), lambda i: (i, (COL_MLA + q_blocks) // MLA_KV_BLOCKS)),
                  full(q_gain.shape), full(kv_gain.shape),
                  full(w1.shape), full(wk.shape), full(wv.shape),
                  pl.BlockSpec((tm, LANE), lambda i: (i % nt, 0)),
                  pl.BlockSpec((tm, LANE), lambda i: (i % nt, 0))],
        out_specs=[pl.BlockSpec((tm, 2 * GROUP_WIDTH), lambda i: (i, 0)),
                   pl.BlockSpec((tm, 2 * GROUP_WIDTH), lambda i: (i, 0)),
                   pl.BlockSpec((tm, GROUP_WIDTH), lambda i: (i, 0))],
        out_shape=[jax.ShapeDtypeStruct((m, 2 * GROUP_WIDTH), BF16),
                   jax.ShapeDtypeStruct((m, 2 * GROUP_WIDTH), BF16),
                   jax.ShapeDtypeStruct((m, GROUP_WIDTH), BF16)],
        compiler_params=_params("arbitrary"),
        name="mla_prep",
    )(front, front, q_gain, kv_gain, w1, wk, wv, cos_t, sin_t)


HALO = 16


def _gdn_prep_kernel(blocks_per_seq, x_ref, halo_ref, ab_ref, cw_ref, alog_ref, dtb_ref,
                     qn_ref, kn_ref, v_ref, gcol_ref, bcol_ref, grow_ref, cat_ref):
    i = pl.program_id(0)
    halo_scale = jnp.where(i % blocks_per_seq == 0, 0.0, 1.0)
    outs = (qn_ref, kn_ref, v_ref)
    rows = x_ref.shape[0]
    for j in range(3 * N_HEADS):
        sl = slice(LANE * j, LANE * (j + 1))
        cat_ref[j, :HALO, :] = halo_ref[:, sl].astype(F32) * halo_scale
        cat_ref[j, HALO:, :] = x_ref[:, sl].astype(F32)
        w = cw_ref[:, sl]
        acc = cat_ref[j, HALO:, :] * w[GDN_CONV - 1:GDN_CONV]
        for s in range(1, GDN_CONV):
            acc = acc + cat_ref[j, HALO - s:HALO - s + rows, :] * w[GDN_CONV - 1 - s:GDN_CONV - s]
        y = _silu(acc)
        part, head = divmod(j, N_HEADS)
        if part < 2:
            norm = lax.rsqrt(jnp.sum(y * y, axis=-1, keepdims=True) + 1e-6)
            y = y * (norm * HEAD_DIM ** -0.5 if part == 0 else norm)
        outs[part][:, LANE * head:LANE * (head + 1)] = y.astype(BF16)

    ab = ab_ref[...].astype(F32)
    z = ab + dtb_ref[...]
    softplus = jnp.maximum(z, 0.0) + jnp.log1p(jnp.exp(-jnp.abs(z)))
    g = -jnp.exp(alog_ref[...]) * softplus
    beta = 1.0 / (1.0 + jnp.exp(-ab))
    row = lax.broadcasted_iota(jnp.int32, g.shape, 0) % CHUNK
    s = 1
    while s < CHUNK:
        g = g + jnp.where(row >= s, pltpu.roll(g, s, axis=0), 0.0)
        s *= 2
    gt = g.T
    for h in range(N_HEADS):
        gcol_ref[h] = jnp.broadcast_to(g[:, h:h + 1], g.shape)
        bcol_ref[h] = jnp.broadcast_to(beta[:, N_HEADS + h:N_HEADS + h + 1], g.shape)
        grow_ref[h] = gt[h:h + 1, :]


def _gdn_prep(front, conv_w, alog, dtb, b, t):
    m = b * t
    tm = GDN_GROUP
    nt = t // tm
    assert OFF_GDN_QKV % (3 * GROUP_WIDTH) == 0
    qkv_block = OFF_GDN_QKV // (3 * GROUP_WIDTH)
    return pl.pallas_call(
        functools.partial(_gdn_prep_kernel, nt),
        grid=(m // tm,),
        in_specs=[pl.BlockSpec((tm, 3 * GROUP_WIDTH), lambda i: (i, qkv_block)),
                  pl.BlockSpec((HALO, 3 * GROUP_WIDTH), lambda i: (jnp.maximum(i * (tm // HALO) - 1, 0), qkv_block)),
                  pl.BlockSpec((tm, LANE), lambda i: (i, COL_GDN_AB)),
                  pl.BlockSpec((GDN_CONV, 3 * GROUP_WIDTH), lambda i: (0, 0)),
                  pl.BlockSpec((1, LANE), lambda i: (0, 0)),
                  pl.BlockSpec((1, LANE), lambda i: (0, 0))],
        out_specs=[pl.BlockSpec((tm, GROUP_WIDTH), lambda i: (i, 0)),
                   pl.BlockSpec((tm, GROUP_WIDTH), lambda i: (i, 0)),
                   pl.BlockSpec((tm, GROUP_WIDTH), lambda i: (i, 0)),
                   pl.BlockSpec((None, N_HEADS, tm, LANE), lambda i: (i // nt, 0, i % nt, 0)),
                   pl.BlockSpec((None, N_HEADS, tm, LANE), lambda i: (i // nt, 0, i % nt, 0)),
                   pl.BlockSpec((None, N_HEADS, 1, tm), lambda i: (i // nt, 0, 0, i % nt))],
        out_shape=[jax.ShapeDtypeStruct((m, GROUP_WIDTH), BF16),
                   jax.ShapeDtypeStruct((m, GROUP_WIDTH), BF16),
                   jax.ShapeDtypeStruct((m, GROUP_WIDTH), BF16),
                   jax.ShapeDtypeStruct((b, N_HEADS, t, LANE), F32),
                   jax.ShapeDtypeStruct((b, N_HEADS, t, LANE), F32),
                   jax.ShapeDtypeStruct((b, N_HEADS, 1, t), F32)],
        scratch_shapes=[pltpu.VMEM((3 * N_HEADS, HALO + tm, LANE), F32)],
        compiler_params=_params("arbitrary"),
        name="gdn_prep",
    )(front, front, front, conv_w, alog, dtb)


GDN_HEADS_PER_STEP = 8


def _gdn_kernel(q_ref, k_ref, v_ref, gc_ref, bc_ref, gr_ref, gate_ref, nw_ref, o_ref, state_ref):
    n = pl.program_id(2)

    @pl.when(n == 0)
    def _():
        state_ref[...] = jnp.zeros_like(state_ref)

    g_sz = GDN_GROUP
    ri = lax.broadcasted_iota(jnp.int32, (g_sz, g_sz), 0)
    ci = lax.broadcasted_iota(jnp.int32, (g_sz, g_sz), 1)
    dif = jnp.where((ri // CHUNK) == (ci // CHUNK), ri - ci, -1)
    heads = range(GDN_HEADS_PER_STEP)
    cols = [slice(HEAD_DIM * hb, HEAD_DIM * (hb + 1)) for hb in heads]
    outs = _gdn_heads(dif, [q_ref[:, sl] for sl in cols], [k_ref[:, sl] for sl in cols],
                      [v_ref[:, sl] for sl in cols], [gc_ref[hb] for hb in heads], [bc_ref[hb] for hb in heads],
                      [gr_ref[hb] for hb in heads], state_ref)
    for hb in heads:
        o = outs[hb]
        o = o * lax.rsqrt(jnp.mean(o * o, axis=-1, keepdims=True) + 1e-6) * nw_ref[...]
        o_ref[:, cols[hb]] = (o * _silu(gate_ref[:, cols[hb]].astype(F32))).astype(o_ref.dtype)


def _gdn_heads(dif, q16, k16, v16, gc, beta, gr, state_ref):
    heads = range(len(q16))
    n_chunks = GDN_GROUP // CHUNK
    incl = dif >= 0
    kf = [k16[h].astype(F32) for h in heads]
    eg = [jnp.exp(gc[h]) for h in heads]
    kbeta = [kf[h] * beta[h] for h in heads]
    decay = [jnp.where(incl, jnp.exp(jnp.where(incl, jnp.concatenate([gc[h], gc[h]], axis=1) - gr[h], 0.0)), 0.0)
             for h in heads]

    kk = [_dot_nt(kbeta[h].astype(BF16), k16[h]) for h in heads]
    mpow = [jnp.where(dif > 0, -(kk[h] * decay[h]), 0.0) for h in heads]
    inv = [jnp.where(dif == 0, 1.0, 0.0) + mpow[h] for h in heads]
    mpow16 = [mpow[h].astype(BF16) for h in heads]
    for _ in range(int(math.log2(CHUNK)) - 1):
        mpow16 = [_dot(mpow16[h], mpow16[h]).astype(BF16) for h in heads]
        inv = [inv[h] + _dot(inv[h].astype(BF16), mpow16[h]) for h in heads]

    rhs = [jnp.concatenate([v16[h].astype(F32) * beta[h], kbeta[h] * eg[h]], axis=1).astype(BF16) for h in heads]
    uw = [_dot(inv[h].astype(BF16), rhs[h]) for h in heads]
    u = [uw[h][:, :HEAD_DIM] for h in heads]
    w = [uw[h][:, HEAD_DIM:].astype(BF16) for h in heads]
    qk = [(_dot_nt(q16[h], k16[h]) * decay[h]).astype(BF16) for h in heads]
    qe = [(q16[h].astype(F32) * eg[h]).astype(BF16) for h in heads]
    g_last_rows = [[gc[h][CHUNK * (c + 1) - 1:CHUNK * (c + 1), :] for c in range(n_chunks)] for h in heads]
    kdec_t = [(kf[h] * jnp.exp(jnp.concatenate([jnp.broadcast_to(g, (CHUNK, LANE)) for g in g_last_rows[h]], axis=0)
                               - gc[h])).T.astype(BF16) for h in heads]
    col_chunk = lax.broadcasted_iota(jnp.int32, kdec_t[0].shape, 1) // CHUNK

    vparts = [[u[h][CHUNK * c:CHUNK * (c + 1)] for c in range(n_chunks)] for h in heads]
    state = [state_ref[h] for h in heads]
    outs = [[] for _ in heads]
    for c in range(n_chunks):
        rows = slice(CHUNK * c, CHUNK * (c + 1))
        s16 = [state[h].astype(BF16) for h in heads]
        ws = [_dot(w[h][rows], s16[h]) for h in heads]
        for h in heads:
            vparts[h][c] = u[h][rows] - ws[h]
        vfull = [jnp.concatenate(vparts[h], axis=0).astype(BF16) for h in heads]
        upd = [_dot(jnp.where(col_chunk == c, kdec_t[h], jnp.zeros_like(kdec_t[h])), vfull[h]) for h in heads]
        state = [state[h] * jnp.exp(g_last_rows[h][c]) + upd[h] for h in heads]
        for h in heads:
            outs[h].append(_dot(qe[h][rows], s16[h]) + _dot(qk[h][rows], vfull[h]))
    for h in heads:
        state_ref[h] = state[h]
    return [jnp.concatenate(outs[h], axis=0) for h in heads]


def _gated_delta_net(qn, kn, vv, gcol, bcol, grow, back, norm_w, b, t):
    assert (COL_GATE + N_HEADS) % GDN_HEADS_PER_STEP == 0
    m = b * t
    ng = t // GDN_GROUP
    hb = GDN_HEADS_PER_STEP
    blk = (GDN_GROUP, HEAD_DIM * hb)
    tok = lambda bi, h, n: (bi * ng + n, h)
    return pl.pallas_call(
        _gdn_kernel,
        grid=(b, N_HEADS // hb, ng),
        in_specs=[pl.BlockSpec(blk, tok), pl.BlockSpec(blk, tok), pl.BlockSpec(blk, tok),
                  pl.BlockSpec((None, hb, GDN_GROUP, LANE), lambda bi, h, n: (bi, h, n, 0)),
                  pl.BlockSpec((None, hb, GDN_GROUP, LANE), lambda bi, h, n: (bi, h, n, 0)),
                  pl.BlockSpec((None, hb, 1, GDN_GROUP), lambda bi, h, n: (bi, h, 0, n)),
                  pl.BlockSpec(blk, lambda bi, h, n: (bi * ng + n, (COL_GATE + N_HEADS) // hb + h)),
                  pl.BlockSpec((1, HEAD_DIM), lambda bi, h, n: (0, 0))],
        out_specs=pl.BlockSpec(blk, tok),
        out_shape=jax.ShapeDtypeStruct((m, GROUP_WIDTH), BF16),
        scratch_shapes=[pltpu.VMEM((hb, HEAD_DIM, HEAD_DIM), F32)],
        compiler_params=_params("arbitrary", "arbitrary", "arbitrary"),
        name="gated_delta_net",
    )(qn, kn, vv, gcol, bcol, grow, back, norm_w)


def _out_proj_kernel(oa_ref, ob_ref, oc_ref, od_ref, w_ref, x_ref, g_ref, b_ref, y_ref, *y16_ref):
    o = jnp.concatenate([oa_ref[...], ob_ref[...], oc_ref[...], od_ref[...]], axis=1)
    z = _dot(o, w_ref[...]) + DEEPNORM_ALPHA * x_ref[...]
    zc = z - jnp.mean(z, axis=-1, keepdims=True)
    y = zc * lax.rsqrt(jnp.mean(zc * zc, axis=-1, keepdims=True) + 1e-5) * g_ref[...] + b_ref[...]
    y_ref[...] = y
    for ref in y16_ref:
        ref[...] = y.astype(BF16)


def _out_proj_ln(oa, ob, oc, od, w16, x2d, gain, bias, with_bf16):
    m = x2d.shape[0]
    tm = 128
    grp = pl.BlockSpec((tm, GROUP_WIDTH), lambda i: (i, 0))
    row = pl.BlockSpec((tm, D_MODEL), lambda i: (i, 0))
    vec = pl.BlockSpec((1, D_MODEL), lambda i: (0, 0))
    weight = pl.BlockSpec((D_MODEL, D_MODEL), lambda i: (0, 0), pipeline_mode=pl.Buffered(1))
    return pl.pallas_call(
        _out_proj_kernel,
        grid=(m // tm,),
        in_specs=[grp, grp, grp, grp, weight, row, vec, vec],
        out_specs=[row, row][:1 + with_bf16],
        out_shape=[jax.ShapeDtypeStruct((m, D_MODEL), F32),
                   jax.ShapeDtypeStruct((m, D_MODEL), BF16)][:1 + with_bf16],
        compiler_params=_params("arbitrary"),
        name="out_proj_layernorm",
    )(oa, ob, oc, od, w16, x2d, gain, bias)


def _place(a, axis, offset, size):
    pad = [(0, 0)] * a.ndim
    pad[axis] = (offset, size - offset - a.shape[axis])
    return jnp.pad(a, pad)


def _layout_w_uq(w):
    w = w.reshape(MLA_Q_RANK, N_HEADS, MLA_NOPE + MLA_ROPE)
    w1 = jnp.concatenate([w[:, :, :MLA_NOPE], _place(w[:, :, MLA_NOPE:], 2, MLA_LANE_OFF, LANE)], axis=2)
    w1 = w1.reshape(MLA_Q_RANK, N_HEADS * 2 * LANE)
    return _place(w1, 0, MLA_LANE_OFF, MLA_Q_BLOCKS * LANE).astype(BF16)


def _layout_w_ukv(w):
    w = w.reshape(MLA_KV_RANK, N_HEADS, MLA_NOPE + HEAD_DIM)
    wk = w[:, :, :MLA_NOPE].reshape(MLA_KV_RANK, GROUP_WIDTH)
    wv = w[:, :, MLA_NOPE:].reshape(MLA_KV_RANK, GROUP_WIDTH)
    rows = MLA_KV_BLOCKS * LANE
    return _place(wk, 0, MLA_LANE_OFF, rows).astype(BF16), _place(wv, 0, MLA_LANE_OFF, rows).astype(BF16)


def _rope_tables(t):
    half = MLA_ROPE // 2
    inv = ROPE_THETA ** (-jnp.arange(half, dtype=F32) / half)
    ang = jnp.arange(t).astype(F32)[:, None] * inv[None, :]
    cos_t = _place(jnp.concatenate([jnp.cos(ang), jnp.cos(ang)], axis=1), 1, MLA_LANE_OFF, LANE)
    sin_t = _place(jnp.concatenate([jnp.sin(ang), jnp.sin(ang)], axis=1), 1, MLA_LANE_OFF, LANE)
    return cos_t, sin_t


def _pad_lanes(v):
    return jnp.concatenate([v.astype(F32), jnp.zeros((LANE - v.shape[0],), F32)])[None, :]


def _layer(x2d, x16, layer_idx, b, t, w_in, diff_lambda, diff_norm, gdn_conv, gdn_a_log, gdn_dt_bias, gdn_norm,
           mla_q_norm, mla_w_uq, mla_kv_norm, mla_w_ukv, rel_bias, w_out, ln_gain, ln_bias, tables):
    front, back = _in_proj(x16, w_in, layer_idx)

    lam_init = 0.8 - 0.6 * math.exp(-0.3 * layer_idx)
    slopes = 2.0 ** (-8.0 * jnp.arange(1, N_HEADS + 1, dtype=F32) / N_HEADS)
    slopes = jnp.broadcast_to(slopes[:, None, None], (N_HEADS, 1, LANE))
    lam_p = jnp.concatenate([diff_lambda.astype(F32), jnp.zeros((4, LANE - DIFF_HALF), F32)], axis=1)
    o_a = _diff_attention(front, back, slopes, lam_p, diff_norm.astype(F32)[None, :], lam_init, b, t)

    qn, kn, vv, gcol, bcol, grow = _gdn_prep(front, gdn_conv.astype(F32), _pad_lanes(gdn_a_log),
                                              _pad_lanes(gdn_dt_bias), b, t)
    o_b = _gated_delta_net(qn, kn, vv, gcol, bcol, grow, back, gdn_norm.astype(F32)[None, :], b, t)

    w1 = _layout_w_uq(mla_w_uq)
    wk, wv = _layout_w_ukv(mla_w_ukv)
    cos_t, sin_t = tables
    q_gain = _place(mla_q_norm.astype(F32)[None, :], 1, MLA_LANE_OFF, MLA_Q_BLOCKS * LANE)
    kv_gain = _place(mla_kv_norm.astype(F32)[None, :], 1, MLA_LANE_OFF, MLA_KV_BLOCKS * LANE)
    qf, kf, vc = _mla_prep(front, q_gain, kv_gain, w1, wk, wv, cos_t, sin_t, b, t)
    o_c = _mla_attention(qf, kf, vc, back, b, t)

    o_d = _band_attention(back, _band_bias_tiles(rel_bias), b, t)

    outs = _out_proj_ln(o_a, o_b, o_c, o_d, w_out.astype(BF16), x2d,
                        ln_gain.astype(F32)[None, :], ln_bias.astype(F32)[None, :], layer_idx + 1 < DEPTH)
    return outs[0], (outs[1] if len(outs) > 1 else None)


def kernel(x, w_in, diff_lambda, diff_norm, gdn_conv, gdn_a_log, gdn_dt_bias, gdn_norm, mla_q_norm, mla_w_uq,
           mla_kv_norm, mla_w_ukv, rel_bias, w_out, ln_gain, ln_bias):
    b, t, d = x.shape
    assert d == D_MODEL and t % GDN_GROUP == 0 and (b * t) % IN_PROJ_ROWS == 0
    tables = _rope_tables(t)
    x2d = x.reshape(b * t, d)
    x16 = x2d.astype(BF16)
    w_t = jnp.swapaxes(w_in, 1, 2)
    for l in range(DEPTH):
        x2d, x16 = _layer(x2d, x16, l, b, t, w_t, diff_lambda[l], diff_norm[l], gdn_conv[l], gdn_a_log[l],
                          gdn_dt_bias[l], gdn_norm[l], mla_q_norm[l], mla_w_uq[l], mla_kv_norm[l],
                          mla_w_ukv[l], rel_bias[l], w_out[l], ln_gain[l], ln_bias[l], tables)
    return x2d.reshape(b, t, d)
```

```python
import functools
import math

import jax
import jax.numpy as jnp
from jax import lax
from jax.experimental import pallas as pl
from jax.experimental.pallas import tpu as pltpu

F32 = jnp.float32
BF16 = jnp.bfloat16

D_MODEL = 4096
DEPTH = 2
CHUNK = 64
N_HEADS = 8
HEAD_DIM = 128
GROUP_WIDTH = N_HEADS * HEAD_DIM
DIFF_HALF = HEAD_DIM // 2
GDN_CONV = 4
MLA_Q_RANK = 768
MLA_KV_RANK = 256
MLA_NOPE = 128
MLA_ROPE = 64
ROPE_THETA = 10000.0
BAND_CHUNKS = 9
REL_CLIP = 128
DEEPNORM_ALPHA = (2 * DEPTH) ** 0.25
IN_SIZES = (GROUP_WIDTH, GROUP_WIDTH, GROUP_WIDTH, 3 * GROUP_WIDTH, N_HEADS, N_HEADS,
            MLA_Q_RANK, MLA_KV_RANK + MLA_ROPE, GROUP_WIDTH, GROUP_WIDTH, GROUP_WIDTH, D_MODEL)

LANE = 128
IN_COLS = sum(IN_SIZES)
(OFF_DIFF_Q, OFF_DIFF_K, OFF_DIFF_V, OFF_GDN_QKV, OFF_GDN_A, OFF_GDN_B, OFF_MLA_Q, OFF_MLA_KV,
 OFF_BAND_Q, OFF_BAND_K, OFF_BAND_V, OFF_GATE) = (sum(IN_SIZES[:n]) for n in range(len(IN_SIZES)))

FRONT_TILE = 768
FRONT_COLS = 10 * FRONT_TILE
BACK_TILE = 512
BACK_SHIFT = OFF_BAND_Q % LANE
BACK_COLS = IN_COLS - OFF_BAND_Q
NEXT_ROWS = LANE
MLA_LANE_OFF = OFF_MLA_Q % LANE
assert OFF_GDN_QKV % LANE == 0 and OFF_GDN_A % LANE == 0 and OFF_GDN_B == OFF_GDN_A + N_HEADS
assert OFF_MLA_KV % LANE == MLA_LANE_OFF and OFF_GATE % LANE == BACK_SHIFT and BACK_COLS % BACK_TILE == 0
assert FRONT_COLS >= OFF_BAND_Q and (OFF_BAND_Q - BACK_SHIFT) % BACK_TILE == 0
COL_DIFF_Q = OFF_DIFF_Q // LANE
COL_DIFF_K = OFF_DIFF_K // LANE
COL_DIFF_V = OFF_DIFF_V // LANE
COL_GDN_QKV = OFF_GDN_QKV // LANE
COL_GDN_AB = OFF_GDN_A // LANE
COL_MLA = OFF_MLA_Q // LANE
COL_BAND_Q = 0
COL_BAND_K = (OFF_BAND_K - OFF_BAND_Q) // LANE
COL_BAND_V = (OFF_BAND_V - OFF_BAND_Q) // LANE
COL_GATE = (OFF_GATE - OFF_BAND_Q) // LANE
MLA_Q_BLOCKS = MLA_Q_RANK // LANE + 1
MLA_KV_BLOCKS = 3

ATT_BLOCK = 256
GDN_GROUP = 256
NEG_BIG = -1e30
VMEM_LIMIT = 48 * 1024 * 1024
IN_PROJ_ROWS = 512
BACK_VMEM_LIMIT = (2 * (BACK_TILE + NEXT_ROWS) * D_MODEL * 4 + BACK_TILE * D_MODEL * 2
                   + 2 * IN_PROJ_ROWS * D_MODEL * 2 + IN_PROJ_ROWS * BACK_TILE * (2 * 2 + 4) + (4 << 20))


def _params(*sem, vmem=VMEM_LIMIT):
    return pltpu.CompilerParams(dimension_semantics=sem, vmem_limit_bytes=vmem)


def _dot(a, b):
    return jnp.dot(a, b, preferred_element_type=F32)


def _dot_nt(a, b):
    return lax.dot_general(a, b, (((1,), (1,)), ((), ())), preferred_element_type=F32)


def _silu(x):
    return x * (1.0 / (1.0 + jnp.exp(-x)))


CAST_ROWS = 16


def _cast_rows(src_ref, src_row, dst_ref, dst_row, n_rows):
    def body(r, carry):
        src = pl.ds(pl.multiple_of(src_row + r * CAST_ROWS, CAST_ROWS), CAST_ROWS)
        dst = pl.ds(pl.multiple_of(dst_row + r * CAST_ROWS, CAST_ROWS), CAST_ROWS)
        dst_ref[dst, :] = src_ref[src, :].astype(BF16)
        return carry
    lax.fori_loop(0, n_rows // CAST_ROWS, body, 0)


def _in_proj_front_kernel(x_ref, w_ref, o_ref, w16_ref):
    @pl.when(pl.program_id(1) == 0)
    def _():
        _cast_rows(w_ref, 0, w16_ref, 0, w_ref.shape[0])

    o_ref[...] = _dot_nt(x_ref[...], w16_ref[...]).astype(o_ref.dtype)


def _in_proj_back_kernel(x_ref, w_ref, w_next_ref, o_ref, w16_ref):
    @pl.when(pl.program_id(1) == 0)
    def _():
        own = w_ref.shape[0] - BACK_SHIFT
        _cast_rows(w_ref, BACK_SHIFT, w16_ref, 0, own)
        _cast_rows(w_next_ref, 0, w16_ref, own, BACK_SHIFT)

    o_ref[...] = _dot_nt(x_ref[...], w16_ref[...]).astype(o_ref.dtype)


def _in_proj(xb, w_t, layer):
    m = xb.shape[0]
    tm = IN_PROJ_ROWS
    x_spec = pl.BlockSpec((tm, D_MODEL), lambda j, i: (i, 0))
    front = pl.pallas_call(
        _in_proj_front_kernel,
        grid=(FRONT_COLS // FRONT_TILE, m // tm),
        in_specs=[x_spec, pl.BlockSpec((None, FRONT_TILE, D_MODEL), lambda j, i: (layer, j, 0))],
        out_specs=pl.BlockSpec((tm, FRONT_TILE), lambda j, i: (i, j)),
        out_shape=jax.ShapeDtypeStruct((m, FRONT_COLS), BF16),
        scratch_shapes=[pltpu.VMEM((FRONT_TILE, D_MODEL), BF16)],
        compiler_params=_params("arbitrary", "arbitrary"),
        name="in_proj_front",
    )(xb, w_t)
    first_tile = (OFF_BAND_Q - BACK_SHIFT) // BACK_TILE
    next_blocks = BACK_TILE // NEXT_ROWS
    back = pl.pallas_call(
        _in_proj_back_kernel,
        grid=(BACK_COLS // BACK_TILE, m // tm),
        in_specs=[x_spec,
                  pl.BlockSpec((None, BACK_TILE, D_MODEL), lambda j, i: (layer, first_tile + j, 0)),
                  pl.BlockSpec((None, NEXT_ROWS, D_MODEL), lambda j, i: (layer, (first_tile + j + 1) * next_blocks, 0))],
        out_specs=pl.BlockSpec((tm, BACK_TILE), lambda j, i: (i, j)),
        out_shape=jax.ShapeDtypeStruct((m, BACK_COLS), BF16),
        scratch_shapes=[pltpu.VMEM((BACK_TILE, D_MODEL), BF16)],
        compiler_params=_params("arbitrary", "arbitrary", vmem=BACK_VMEM_LIMIT),
        name="in_proj_back",
    )(xb, w_t, w_t)
    return front, back


def _softmax_stats(m, l, s, scale, bias, shift):
    t = s if scale == 1.0 else s * scale
    if bias is not None:
        t = t + bias
    m_new = jnp.maximum(m, jnp.max(t, axis=0, keepdims=True) + shift)
    alpha = jnp.exp(m - m_new)
    p = jnp.exp(t - (m_new - shift))
    return m_new, alpha * l + jnp.sum(p, axis=0, keepdims=True), alpha, p.astype(BF16)


DIFF_HEADS_PER_STEP = 4
MLA_HEADS_PER_STEP = 8
BAND_HEADS_PER_STEP = 8


def _load_values_transposed(step, v_ref, vt_ref):
    @pl.when(step == 0)
    def _():
        for h in range(vt_ref.shape[0]):
            for j in range(vt_ref.shape[1]):
                v = v_ref[ATT_BLOCK * j:ATT_BLOCK * (j + 1), HEAD_DIM * h:HEAD_DIM * (h + 1)]
                vt_ref[h, j] = v.astype(F32).T.astype(BF16)


def _attn_sweep(qs, k_ref, vt_ref, lo, i, off_scores, diag_scores):
    heads = range(len(qs))
    nq, dk = qs[0].shape

    def keys(kb, h):
        return k_ref[pl.ds(pl.multiple_of(kb * ATT_BLOCK, ATT_BLOCK), ATT_BLOCK), dk * h:dk * (h + 1)]

    def body(kb, carry):
        s = [_dot_nt(keys(kb, h), qs[h]) for h in heads]
        prev = jnp.maximum(kb - 1, lo)
        acc = [carry[h][4] * carry[h][2] + _dot(vt_ref[h, prev], carry[h][3]) for h in heads]
        stats = [_softmax_stats(carry[h][0], carry[h][1], s[h], *off_scores(kb, h)) for h in heads]
        return tuple((stats[h][0], stats[h][1], acc[h], stats[h][3], stats[h][2]) for h in heads)

    init = tuple((jnp.full((1, nq), NEG_BIG, F32), jnp.zeros((1, nq), F32), jnp.zeros((HEAD_DIM, nq), F32),
                  jnp.zeros((ATT_BLOCK, nq), BF16), jnp.ones((1, nq), F32)) for h in heads)
    carry = lax.fori_loop(lo, i, body, init)
    s = [_dot_nt(keys(i, h), qs[h]) for h in heads]
    prev = jnp.maximum(i - 1, lo)
    acc = [carry[h][4] * carry[h][2] + _dot(vt_ref[h, prev], carry[h][3]) for h in heads]
    stats = [_softmax_stats(carry[h][0], carry[h][1], s[h], *diag_scores(h), 0.0) for h in heads]
    return [(stats[h][2] * acc[h] + _dot(vt_ref[h, i], stats[h][3]), stats[h][1]) for h in heads]


def _block_iotas():
    key = lax.broadcasted_iota(jnp.int32, (ATT_BLOCK, ATT_BLOCK), 0)
    qry = lax.broadcasted_iota(jnp.int32, (ATT_BLOCK, ATT_BLOCK), 1)
    return key, qry


def _diff_attn_kernel(lam_init, q_ref, k_ref, v_ref, gate_ref, slope_ref, lam_ref, nw_ref, o_ref, vt_ref):
    i = pl.program_id(2)
    heads = range(DIFF_HEADS_PER_STEP)
    cols = [slice(HEAD_DIM * h, HEAD_DIM * (h + 1)) for h in heads]
    key, qry = _block_iotas()
    dist = (qry - key).astype(F32)
    visible = (key // CHUNK) <= (qry // CHUNK)
    qs, slopes, bias_off, bias_diag = [], [], [], []
    for h in heads:
        q = q_ref[:, cols[h]] * (DIFF_HALF ** -0.5)
        lane = lax.broadcasted_iota(jnp.int32, q.shape, 1)
        qs.append(jnp.concatenate([jnp.where(lane < DIFF_HALF, q, jnp.zeros_like(q)),
                                   jnp.where(lane >= DIFF_HALF, q, jnp.zeros_like(q))], axis=0))
        slope = slope_ref[h][:, 0:1]
        slopes.append(slope)
        off = -slope * dist
        diag = jnp.where(visible, -slope * jnp.abs(dist), NEG_BIG)
        bias_off.append(jnp.concatenate([off, off], axis=1))
        bias_diag.append(jnp.concatenate([diag, diag], axis=1))

    def off_scores(kb, h):
        return 1.0, bias_off[h], -slopes[h] * ((i - kb) * ATT_BLOCK).astype(F32)

    def diag_scores(h):
        return 1.0, bias_diag[h]

    _load_values_transposed(i, v_ref, vt_ref)
    outs = _attn_sweep(qs, k_ref, vt_ref, 0, i, off_scores, diag_scores)
    lp = lam_ref[...]
    lam = (jnp.exp(jnp.sum(lp[0:1] * lp[1:2], axis=-1, keepdims=True))
           - jnp.exp(jnp.sum(lp[2:3] * lp[3:4], axis=-1, keepdims=True)) + lam_init)
    for h in heads:
        acc, l = outs[h]
        o = acc / l
        o = (o[:, :ATT_BLOCK] - lam * o[:, ATT_BLOCK:]).T
        o = o * lax.rsqrt(jnp.mean(o * o, axis=-1, keepdims=True) + 1e-6) * nw_ref[...]
        o = o * (1.0 - lam_init)
        o_ref[:, cols[h]] = (o * _silu(gate_ref[:, cols[h]].astype(F32))).astype(o_ref.dtype)


def _diff_attention(front, back, slopes, lam_p, norm_w, lam_init, b, t):
    nq = t // ATT_BLOCK
    m = b * t
    hp = DIFF_HEADS_PER_STEP
    assert COL_DIFF_Q % hp == 0 and COL_DIFF_K % hp == 0 and COL_DIFF_V % hp == 0 and COL_GATE % hp == 0
    blk = (ATT_BLOCK, hp * HEAD_DIM)
    return pl.pallas_call(
        functools.partial(_diff_attn_kernel, lam_init),
        grid=(b, N_HEADS // hp, nq),
        in_specs=[pl.BlockSpec(blk, lambda bi, h, i: (bi * nq + i, COL_DIFF_Q // hp + h)),
                  pl.BlockSpec((t, hp * HEAD_DIM), lambda bi, h, i: (bi, COL_DIFF_K // hp + h)),
                  pl.BlockSpec((t, hp * HEAD_DIM), lambda bi, h, i: (bi, COL_DIFF_V // hp + h)),
                  pl.BlockSpec(blk, lambda bi, h, i: (bi * nq + i, COL_GATE // hp + h)),
                  pl.BlockSpec((hp, 1, LANE), lambda bi, h, i: (h, 0, 0)),
                  pl.BlockSpec((4, LANE), lambda bi, h, i: (0, 0)),
                  pl.BlockSpec((1, HEAD_DIM), lambda bi, h, i: (0, 0))],
        out_specs=pl.BlockSpec(blk, lambda bi, h, i: (bi * nq + i, h)),
        out_shape=jax.ShapeDtypeStruct((m, GROUP_WIDTH), BF16),
        scratch_shapes=[_values_scratch(t, hp)],
        compiler_params=_params("arbitrary", "arbitrary", "arbitrary"),
        name="diff_attention",
    )(front, front, front, back, slopes, lam_p, norm_w)


def _mla_attn_kernel(q_ref, k_ref, v_ref, gate_ref, o_ref, vt_ref):
    i = pl.program_id(2)
    key, qry = _block_iotas()
    mask_diag = jnp.where((key // CHUNK) <= (qry // CHUNK), 0.0, NEG_BIG)
    scale = (MLA_NOPE + MLA_ROPE) ** -0.5

    def off_scores(kb, h):
        return scale, None, 0.0

    def diag_scores(h):
        return scale, mask_diag

    _load_values_transposed(i, v_ref, vt_ref)
    dk = 2 * LANE
    qs = [q_ref[:, dk * h:dk * (h + 1)] for h in range(MLA_HEADS_PER_STEP)]
    outs = _attn_sweep(qs, k_ref, vt_ref, 0, i, off_scores, diag_scores)
    _store_gated(outs, gate_ref, o_ref)


def _store_gated(outs, gate_ref, o_ref):
    for h, (acc, l) in enumerate(outs):
        cols = slice(HEAD_DIM * h, HEAD_DIM * (h + 1))
        o_ref[:, cols] = ((acc / l).T * _silu(gate_ref[:, cols].astype(F32))).astype(o_ref.dtype)


def _mla_attention(qf, kf, vv, back, b, t):
    nq = t // ATT_BLOCK
    m = b * t
    hp = MLA_HEADS_PER_STEP
    assert (COL_GATE + 2 * N_HEADS) % hp == 0
    return pl.pallas_call(
        _mla_attn_kernel,
        grid=(b, N_HEADS // hp, nq),
        in_specs=[pl.BlockSpec((ATT_BLOCK, hp * 2 * LANE), lambda bi, h, i: (bi * nq + i, h)),
                  pl.BlockSpec((t, hp * 2 * LANE), lambda bi, h, i: (bi, h)),
                  pl.BlockSpec((t, hp * HEAD_DIM), lambda bi, h, i: (bi, h)),
                  pl.BlockSpec((ATT_BLOCK, hp * HEAD_DIM),
                               lambda bi, h, i: (bi * nq + i, (COL_GATE + 2 * N_HEADS) // hp + h))],
        out_specs=pl.BlockSpec((ATT_BLOCK, hp * HEAD_DIM), lambda bi, h, i: (bi * nq + i, h)),
        out_shape=jax.ShapeDtypeStruct((m, GROUP_WIDTH), BF16),
        scratch_shapes=[_values_scratch(t, hp)],
        compiler_params=_params("arbitrary", "arbitrary", "arbitrary"),
        name="mla_attention",
    )(qf, kf, vv, back)


BAND_KEY_BLOCKS = (BAND_CHUNKS - 1) * CHUNK // ATT_BLOCK + 1


def _values_scratch(t, heads):
    return pltpu.VMEM((heads, t // ATT_BLOCK, HEAD_DIM, ATT_BLOCK), BF16)


def _band_attn_kernel(q_ref, k_ref, v_ref, gate_ref, bias_ref, o_ref, vt_ref):
    i = pl.program_id(2)
    scale = HEAD_DIM ** -0.5

    def off_scores(kb, h):
        return scale, bias_ref[h, kb - i + (BAND_KEY_BLOCKS - 1)], 0.0

    def diag_scores(h):
        return scale, bias_ref[h, BAND_KEY_BLOCKS - 1]

    lo = jnp.maximum(i - (BAND_KEY_BLOCKS - 1), 0)
    _load_values_transposed(i, v_ref, vt_ref)
    qs = [q_ref[:, HEAD_DIM * h:HEAD_DIM * (h + 1)] for h in range(BAND_HEADS_PER_STEP)]
    outs = _attn_sweep(qs, k_ref, vt_ref, lo, i, off_scores, diag_scores)
    _store_gated(outs, gate_ref, o_ref)


def _band_attention(back, bias_tiles, b, t):
    nq = t // ATT_BLOCK
    m = b * t
    hp = BAND_HEADS_PER_STEP
    assert COL_BAND_Q % hp == 0 and COL_BAND_K % hp == 0 and COL_BAND_V % hp == 0
    assert (COL_GATE + 3 * N_HEADS) % hp == 0
    blk = (ATT_BLOCK, hp * HEAD_DIM)
    return pl.pallas_call(
        _band_attn_kernel,
        grid=(b, N_HEADS // hp, nq),
        in_specs=[pl.BlockSpec(blk, lambda bi, h, i: (bi * nq + i, COL_BAND_Q // hp + h)),
                  pl.BlockSpec((t, hp * HEAD_DIM), lambda bi, h, i: (bi, COL_BAND_K // hp + h)),
                  pl.BlockSpec((t, hp * HEAD_DIM), lambda bi, h, i: (bi, COL_BAND_V // hp + h)),
                  pl.BlockSpec(blk, lambda bi, h, i: (bi * nq + i, (COL_GATE + 3 * N_HEADS) // hp + h)),
                  pl.BlockSpec((hp, BAND_KEY_BLOCKS, ATT_BLOCK, ATT_BLOCK), lambda bi, h, i: (h, 0, 0, 0))],
        out_specs=pl.BlockSpec(blk, lambda bi, h, i: (bi * nq + i, h)),
        out_shape=jax.ShapeDtypeStruct((m, GROUP_WIDTH), BF16),
        scratch_shapes=[_values_scratch(t, hp)],
        compiler_params=_params("arbitrary", "arbitrary", "arbitrary"),
        name="band_attention",
    )(back, back, back, back, bias_tiles)


def _band_bias_tiles(rel_bias):
    n = ATT_BLOCK
    back = (BAND_KEY_BLOCKS - 1) - jnp.arange(BAND_KEY_BLOCKS)
    j = jnp.arange(2 * n)
    q_minus_k = jnp.where(j < n, j, j - 2 * n)
    rel = back[:, None] * n + q_minus_k[None, :]
    vals = rel_bias.astype(F32)[:, jnp.clip(rel, -REL_CLIP, REL_CLIP) + REL_CLIP]
    flat = jnp.tile(vals, (1, 1, n))[:, :, :n * (2 * n - 1)]
    tiles = flat.reshape(N_HEADS, BAND_KEY_BLOCKS, n, 2 * n - 1)[:, :, :, :n]
    key = jnp.arange(n)[None, :, None]
    qry = jnp.arange(n)[None, None, :]
    chunk_back = back[:, None, None] * (n // CHUNK) + qry // CHUNK - key // CHUNK
    visible = (chunk_back >= 0) & (chunk_back < BAND_CHUNKS)
    return jnp.where(visible[None], tiles, NEG_BIG)


def _rope_block(x, ct, st):
    half = MLA_ROPE // 2
    lane = lax.broadcasted_iota(jnp.int32, (1, LANE), 1)
    rot = jnp.where(lane < MLA_LANE_OFF + half, -pltpu.roll(x, LANE - half, axis=1), pltpu.roll(x, half, axis=1))
    return x * ct + rot * st


def _mla_prep_kernel(cdq_ref, ckv_ref, qg_ref, kvg_ref, w1_ref, wk_ref, wv_ref, ct_ref, st_ref,
                     q_out, k_out, v_out):
    ct = ct_ref[...]
    st = st_ref[...]
    ckv = ckv_ref[...].astype(F32)
    cq = jnp.concatenate([cdq_ref[...].astype(F32), ckv[:, :LANE]], axis=1)
    lane_q = lax.broadcasted_iota(jnp.int32, (1, cq.shape[1]), 1)
    cq = jnp.where((lane_q >= MLA_LANE_OFF) & (lane_q < MLA_LANE_OFF + MLA_Q_RANK), cq, 0.0)
    ms_q = jnp.sum(cq * cq, axis=-1, keepdims=True) * (1.0 / MLA_Q_RANK)
    nq = (cq * lax.rsqrt(ms_q + 1e-6) * qg_ref[...]).astype(BF16)
    q1 = _dot(nq, w1_ref[...])
    lane_kv = lax.broadcasted_iota(jnp.int32, (1, ckv.shape[1]), 1)
    lat = jnp.where((lane_kv >= MLA_LANE_OFF) & (lane_kv < MLA_LANE_OFF + MLA_KV_RANK), ckv, 0.0)
    ms_kv = jnp.sum(lat * lat, axis=-1, keepdims=True) * (1.0 / MLA_KV_RANK)
    nkv = (lat * lax.rsqrt(ms_kv + 1e-6) * kvg_ref[...]).astype(BF16)
    kn = _dot(nkv, wk_ref[...])
    v_out[...] = _dot(nkv, wv_ref[...]).astype(v_out.dtype)
    kr = _rope_block(ckv[:, 2 * LANE:], ct, st).astype(k_out.dtype)
    for h in range(N_HEADS):
        a, bq = 2 * LANE * h, LANE * h
        q_out[:, a:a + LANE] = q1[:, a:a + LANE].astype(q_out.dtype)
        q_out[:, a + LANE:a + 2 * LANE] = _rope_block(q1[:, a + LANE:a + 2 * LANE], ct, st).astype(q_out.dtype)
        k_out[:, a:a + LANE] = kn[:, bq:bq + LANE].astype(k_out.dtype)
        k_out[:, a + LANE:a + 2 * LANE] = kr


def _mla_prep(front, q_gain, kv_gain, w1, wk, wv, cos_t, sin_t, b, t):
    m = b * t
    tm = 256
    nt = t // tm
    full = lambda shape: pl.BlockSpec(shape, lambda i: (0, 0))
    q_blocks = MLA_Q_BLOCKS - 1
    assert COL_MLA % q_blocks == 0 and (COL_MLA + q_blocks) % MLA_KV_BLOCKS == 0
    return pl.pallas_call(
        _mla_prep_kernel,
        grid=(m // tm,),
        in_specs=[pl.BlockSpec((tm, q_blocks * LANE), lambda i: (i, COL_MLA // q_blocks)),
                  pl.BlockSpec((tm, MLA_KV_BLOCKS * LANE), lambda i: (i, (COL_MLA + q_blocks) // MLA_KV_BLOCKS)),
                  full(q_gain.shape), full(kv_gain.shape),
                  full(w1.shape), full(wk.shape), full(wv.shape),
                  pl.BlockSpec((tm, LANE), lambda i: (i % nt, 0)),
                  pl.BlockSpec((tm, LANE), lambda i: (i % nt, 0))],
        out_specs=[pl.BlockSpec((tm, 2 * GROUP_WIDTH), lambda i: (i, 0)),
                   pl.BlockSpec((tm, 2 * GROUP_WIDTH), lambda i: (i, 0)),
                   pl.BlockSpec((tm, GROUP_WIDTH), lambda i: (i, 0))],
        out_shape=[jax.ShapeDtypeStruct((m, 2 * GROUP_WIDTH), BF16),
                   jax.ShapeDtypeStruct((m, 2 * GROUP_WIDTH), BF16),
                   jax.ShapeDtypeStruct((m, GROUP_WIDTH), BF16)],
        compiler_params=_params("arbitrary"),
        name="mla_prep",
    )(front, front, q_gain, kv_gain, w1, wk, wv, cos_t, sin_t)


HALO = 16


def _gdn_prep_kernel(blocks_per_seq, x_ref, halo_ref, ab_ref, cw_ref, alog_ref, dtb_ref,
                     qn_ref, kn_ref, v_ref, gcol_ref, bcol_ref, grow_ref, cat_ref):
    i = pl.program_id(0)
    halo_scale = jnp.where(i % blocks_per_seq == 0, 0.0, 1.0)
    outs = (qn_ref, kn_ref, v_ref)
    rows = x_ref.shape[0]
    for j in range(3 * N_HEADS):
        sl = slice(LANE * j, LANE * (j + 1))
        cat_ref[j, :HALO, :] = halo_ref[:, sl].astype(F32) * halo_scale
        cat_ref[j, HALO:, :] = x_ref[:, sl].astype(F32)
        w = cw_ref[:, sl]
        acc = cat_ref[j, HALO:, :] * w[GDN_CONV - 1:GDN_CONV]
        for s in range(1, GDN_CONV):
            acc = acc + cat_ref[j, HALO - s:HALO - s + rows, :] * w[GDN_CONV - 1 - s:GDN_CONV - s]
        y = _silu(acc)
        part, head = divmod(j, N_HEADS)
        if part < 2:
            norm = lax.rsqrt(jnp.sum(y * y, axis=-1, keepdims=True) + 1e-6)
            y = y * (norm * HEAD_DIM ** -0.5 if part == 0 else norm)
        outs[part][:, LANE * head:LANE * (head + 1)] = y.astype(BF16)

    ab = ab_ref[...].astype(F32)
    z = ab + dtb_ref[...]
    softplus = jnp.maximum(z, 0.0) + jnp.log1p(jnp.exp(-jnp.abs(z)))
    g = -jnp.exp(alog_ref[...]) * softplus
    beta = 1.0 / (1.0 + jnp.exp(-ab))
    row = lax.broadcasted_iota(jnp.int32, g.shape, 0) % CHUNK
    s = 1
    while s < CHUNK:
        g = g + jnp.where(row >= s, pltpu.roll(g, s, axis=0), 0.0)
        s *= 2
    gt = g.T
    for h in range(N_HEADS):
        gcol_ref[h] = jnp.broadcast_to(g[:, h:h + 1], g.shape)
        bcol_ref[h] = jnp.broadcast_to(beta[:, N_HEADS + h:N_HEADS + h + 1], g.shape)
        grow_ref[h] = gt[h:h + 1, :]


def _gdn_prep(front, conv_w, alog, dtb, b, t):
    m = b * t
    tm = GDN_GROUP
    nt = t // tm
    assert OFF_GDN_QKV % (3 * GROUP_WIDTH) == 0
    qkv_block = OFF_GDN_QKV // (3 * GROUP_WIDTH)
    return pl.pallas_call(
        functools.partial(_gdn_prep_kernel, nt),
        grid=(m // tm,),
        in_specs=[pl.BlockSpec((tm, 3 * GROUP_WIDTH), lambda i: (i, qkv_block)),
                  pl.BlockSpec((HALO, 3 * GROUP_WIDTH), lambda i: (jnp.maximum(i * (tm // HALO) - 1, 0), qkv_block)),
                  pl.BlockSpec((tm, LANE), lambda i: (i, COL_GDN_AB)),
                  pl.BlockSpec((GDN_CONV, 3 * GROUP_WIDTH), lambda i: (0, 0)),
                  pl.BlockSpec((1, LANE), lambda i: (0, 0)),
                  pl.BlockSpec((1, LANE), lambda i: (0, 0))],
        out_specs=[pl.BlockSpec((tm, GROUP_WIDTH), lambda i: (i, 0)),
                   pl.BlockSpec((tm, GROUP_WIDTH), lambda i: (i, 0)),
                   pl.BlockSpec((tm, GROUP_WIDTH), lambda i: (i, 0)),
                   pl.BlockSpec((None, N_HEADS, tm, LANE), lambda i: (i // nt, 0, i % nt, 0)),
                   pl.BlockSpec((None, N_HEADS, tm, LANE), lambda i: (i // nt, 0, i % nt, 0)),
                   pl.BlockSpec((None, N_HEADS, 1, tm), lambda i: (i // nt, 0, 0, i % nt))],
        out_shape=[jax.ShapeDtypeStruct((m, GROUP_WIDTH), BF16),
                   jax.ShapeDtypeStruct((m, GROUP_WIDTH), BF16),
                   jax.ShapeDtypeStruct((m, GROUP_WIDTH), BF16),
                   jax.ShapeDtypeStruct((b, N_HEADS, t, LANE), F32),
                   jax.ShapeDtypeStruct((b, N_HEADS, t, LANE), F32),
                   jax.ShapeDtypeStruct((b, N_HEADS, 1, t), F32)],
        scratch_shapes=[pltpu.VMEM((3 * N_HEADS, HALO + tm, LANE), F32)],
        compiler_params=_params("arbitrary"),
        name="gdn_prep",
    )(front, front, front, conv_w, alog, dtb)


GDN_HEADS_PER_STEP = 8


def _gdn_kernel(q_ref, k_ref, v_ref, gc_ref, bc_ref, gr_ref, gate_ref, nw_ref, o_ref, state_ref):
    n = pl.program_id(2)

    @pl.when(n == 0)
    def _():
        state_ref[...] = jnp.zeros_like(state_ref)

    g_sz = GDN_GROUP
    ri = lax.broadcasted_iota(jnp.int32, (g_sz, g_sz), 0)
    ci = lax.broadcasted_iota(jnp.int32, (g_sz, g_sz), 1)
    dif = jnp.where((ri // CHUNK) == (ci // CHUNK), ri - ci, -1)
    heads = range(GDN_HEADS_PER_STEP)
    cols = [slice(HEAD_DIM * hb, HEAD_DIM * (hb + 1)) for hb in heads]
    outs = _gdn_heads(dif, [q_ref[:, sl] for sl in cols], [k_ref[:, sl] for sl in cols],
                      [v_ref[:, sl] for sl in cols], [gc_ref[hb] for hb in heads], [bc_ref[hb] for hb in heads],
                      [gr_ref[hb] for hb in heads], state_ref)
    for hb in heads:
        o = outs[hb]
        o = o * lax.rsqrt(jnp.mean(o * o, axis=-1, keepdims=True) + 1e-6) * nw_ref[...]
        o_ref[:, cols[hb]] = (o * _silu(gate_ref[:, cols[hb]].astype(F32))).astype(o_ref.dtype)


def _gdn_heads(dif, q16, k16, v16, gc, beta, gr, state_ref):
    heads = range(len(q16))
    n_chunks = GDN_GROUP // CHUNK
    incl = dif >= 0
    kf = [k16[h].astype(F32) for h in heads]
    eg = [jnp.exp(gc[h]) for h in heads]
    kbeta = [kf[h] * beta[h] for h in heads]
    decay = [jnp.where(incl, jnp.exp(jnp.where(incl, jnp.concatenate([gc[h], gc[h]], axis=1) - gr[h], 0.0)), 0.0)
             for h in heads]

    kk = [_dot_nt(kbeta[h].astype(BF16), k16[h]) for h in heads]
    mpow = [jnp.where(dif > 0, -(kk[h] * decay[h]), 0.0) for h in heads]
    inv = [jnp.where(dif == 0, 1.0, 0.0) + mpow[h] for h in heads]
    mpow16 = [mpow[h].astype(BF16) for h in heads]
    for _ in range(int(math.log2(CHUNK)) - 1):
        mpow16 = [_dot(mpow16[h], mpow16[h]).astype(BF16) for h in heads]
        inv = [inv[h] + _dot(inv[h].astype(BF16), mpow16[h]) for h in heads]

    rhs = [jnp.concatenate([v16[h].astype(F32) * beta[h], kbeta[h] * eg[h]], axis=1).astype(BF16) for h in heads]
    uw = [_dot(inv[h].astype(BF16), rhs[h]) for h in heads]
    u = [uw[h][:, :HEAD_DIM] for h in heads]
    w = [uw[h][:, HEAD_DIM:].astype(BF16) for h in heads]
    qk = [(_dot_nt(q16[h], k16[h]) * decay[h]).astype(BF16) for h in heads]
    qe = [(q16[h].astype(F32) * eg[h]).astype(BF16) for h in heads]
    g_last_rows = [[gc[h][CHUNK * (c + 1) - 1:CHUNK * (c + 1), :] for c in range(n_chunks)] for h in heads]
    kdec_t = [(kf[h] * jnp.exp(jnp.concatenate([jnp.broadcast_to(g, (CHUNK, LANE)) for g in g_last_rows[h]], axis=0)
                               - gc[h])).T.astype(BF16) for h in heads]
    col_chunk = lax.broadcasted_iota(jnp.int32, kdec_t[0].shape, 1) // CHUNK

    vparts = [[u[h][CHUNK * c:CHUNK * (c + 1)] for c in range(n_chunks)] for h in heads]
    state = [state_ref[h] for h in heads]
    outs = [[] for _ in heads]
    for c in range(n_chunks):
        rows = slice(CHUNK * c, CHUNK * (c + 1))
        s16 = [state[h].astype(BF16) for h in heads]
        ws = [_dot(w[h][rows], s16[h]) for h in heads]
        for h in heads:
            vparts[h][c] = u[h][rows] - ws[h]
        vfull = [jnp.concatenate(vparts[h], axis=0).astype(BF16) for h in heads]
        upd = [_dot(jnp.where(col_chunk == c, kdec_t[h], jnp.zeros_like(kdec_t[h])), vfull[h]) for h in heads]
        state = [state[h] * jnp.exp(g_last_rows[h][c]) + upd[h] for h in heads]
        for h in heads:
            outs[h].append(_dot(qe[h][rows], s16[h]) + _dot(qk[h][rows], vfull[h]))
    for h in heads:
        state_ref[h] = state[h]
    return [jnp.concatenate(outs[h], axis=0) for h in heads]


def _gated_delta_net(qn, kn, vv, gcol, bcol, grow, back, norm_w, b, t):
    assert (COL_GATE + N_HEADS) % GDN_HEADS_PER_STEP == 0
    m = b * t
    ng = t // GDN_GROUP
    hb = GDN_HEADS_PER_STEP
    blk = (GDN_GROUP, HEAD_DIM * hb)
    tok = lambda bi, h, n: (bi * ng + n, h)
    return pl.pallas_call(
        _gdn_kernel,
        grid=(b, N_HEADS // hb, ng),
        in_specs=[pl.BlockSpec(blk, tok), pl.BlockSpec(blk, tok), pl.BlockSpec(blk, tok),
                  pl.BlockSpec((None, hb, GDN_GROUP, LANE), lambda bi, h, n: (bi, h, n, 0)),
                  pl.BlockSpec((None, hb, GDN_GROUP, LANE), lambda bi, h, n: (bi, h, n, 0)),
                  pl.BlockSpec((None, hb, 1, GDN_GROUP), lambda bi, h, n: (bi, h, 0, n)),
                  pl.BlockSpec(blk, lambda bi, h, n: (bi * ng + n, (COL_GATE + N_HEADS) // hb + h)),
                  pl.BlockSpec((1, HEAD_DIM), lambda bi, h, n: (0, 0))],
        out_specs=pl.BlockSpec(blk, tok),
        out_shape=jax.ShapeDtypeStruct((m, GROUP_WIDTH), BF16),
        scratch_shapes=[pltpu.VMEM((hb, HEAD_DIM, HEAD_DIM), F32)],
        compiler_params=_params("arbitrary", "arbitrary", "arbitrary"),
        name="gated_delta_net",
    )(qn, kn, vv, gcol, bcol, grow, back, norm_w)


def _out_proj_kernel(oa_ref, ob_ref, oc_ref, od_ref, w_ref, x_ref, g_ref, b_ref, y_ref, *y16_ref):
    o = jnp.concatenate([oa_ref[...], ob_ref[...], oc_ref[...], od_ref[...]], axis=1)
    z = _dot(o, w_ref[...]) + DEEPNORM_ALPHA * x_ref[...]
    zc = z - jnp.mean(z, axis=-1, keepdims=True)
    y = zc * lax.rsqrt(jnp.mean(zc * zc, axis=-1, keepdims=True) + 1e-5) * g_ref[...] + b_ref[...]
    y_ref[...] = y
    for ref in y16_ref:
        ref[...] = y.astype(BF16)


def _out_proj_ln(oa, ob, oc, od, w16, x2d, gain, bias, with_bf16):
    m = x2d.shape[0]
    tm = 128
    grp = pl.BlockSpec((tm, GROUP_WIDTH), lambda i: (i, 0))
    row = pl.BlockSpec((tm, D_MODEL), lambda i: (i, 0))
    vec = pl.BlockSpec((1, D_MODEL), lambda i: (0, 0))
    weight = pl.BlockSpec((D_MODEL, D_MODEL), lambda i: (0, 0), pipeline_mode=pl.Buffered(1))
    return pl.pallas_call(
        _out_proj_kernel,
        grid=(m // tm,),
        in_specs=[grp, grp, grp, grp, weight, row, vec, vec],
        out_specs=[row, row][:1 + with_bf16],
        out_shape=[jax.ShapeDtypeStruct((m, D_MODEL), F32),
                   jax.ShapeDtypeStruct((m, D_MODEL), BF16)][:1 + with_bf16],
        compiler_params=_params("arbitrary"),
        name="out_proj_layernorm",
    )(oa, ob, oc, od, w16, x2d, gain, bias)


def _place(a, axis, offset, size):
    pad = [(0, 0)] * a.ndim
    pad[axis] = (offset, size - offset - a.shape[axis])
    return jnp.pad(a, pad)


def _layout_w_uq(w):
    w = w.reshape(MLA_Q_RANK, N_HEADS, MLA_NOPE + MLA_ROPE)
    w1 = jnp.concatenate([w[:, :, :MLA_NOPE], _place(w[:, :, MLA_NOPE:], 2, MLA_LANE_OFF, LANE)], axis=2)
    w1 = w1.reshape(MLA_Q_RANK, N_HEADS * 2 * LANE)
    return _place(w1, 0, MLA_LANE_OFF, MLA_Q_BLOCKS * LANE).astype(BF16)


def _layout_w_ukv(w):
    w = w.reshape(MLA_KV_RANK, N_HEADS, MLA_NOPE + HEAD_DIM)
    wk = w[:, :, :MLA_NOPE].reshape(MLA_KV_RANK, GROUP_WIDTH)
    wv = w[:, :, MLA_NOPE:].reshape(MLA_KV_RANK, GROUP_WIDTH)
    rows = MLA_KV_BLOCKS * LANE
    return _place(wk, 0, MLA_LANE_OFF, rows).astype(BF16), _place(wv, 0, MLA_LANE_OFF, rows).astype(BF16)


def _rope_tables(t):
    half = MLA_ROPE // 2
    inv = ROPE_THETA ** (-jnp.arange(half, dtype=F32) / half)
    ang = jnp.arange(t).astype(F32)[:, None] * inv[None, :]
    cos_t = _place(jnp.concatenate([jnp.cos(ang), jnp.cos(ang)], axis=1), 1, MLA_LANE_OFF, LANE)
    sin_t = _place(jnp.concatenate([jnp.sin(ang), jnp.sin(ang)], axis=1), 1, MLA_LANE_OFF, LANE)
    return cos_t, sin_t


def _pad_lanes(v):
    return jnp.concatenate([v.astype(F32), jnp.zeros((LANE - v.shape[0],), F32)])[None, :]


def _layer(x2d, x16, layer_idx, b, t, w_in, diff_lambda, diff_norm, gdn_conv, gdn_a_log, gdn_dt_bias, gdn_norm,
           mla_q_norm, mla_w_uq, mla_kv_norm, mla_w_ukv, rel_bias, w_out, ln_gain, ln_bias, tables):
    front, back = _in_proj(x16, w_in, layer_idx)

    lam_init = 0.8 - 0.6 * math.exp(-0.3 * layer_idx)
    slopes = 2.0 ** (-8.0 * jnp.arange(1, N_HEADS + 1, dtype=F32) / N_HEADS)
    slopes = jnp.broadcast_to(slopes[:, None, None], (N_HEADS, 1, LANE))
    lam_p = jnp.concatenate([diff_lambda.astype(F32), jnp.zeros((4, LANE - DIFF_HALF), F32)], axis=1)
    o_a = _diff_attention(front, back, slopes, lam_p, diff_norm.astype(F32)[None, :], lam_init, b, t)

    qn, kn, vv, gcol, bcol, grow = _gdn_prep(front, gdn_conv.astype(F32), _pad_lanes(gdn_a_log),
                                              _pad_lanes(gdn_dt_bias), b, t)
    o_b = _gated_delta_net(qn, kn, vv, gcol, bcol, grow, back, gdn_norm.astype(F32)[None, :], b, t)

    w1 = _layout_w_uq(mla_w_uq)
    wk, wv = _layout_w_ukv(mla_w_ukv)
    cos_t, sin_t = tables
    q_gain = _place(mla_q_norm.astype(F32)[None, :], 1, MLA_LANE_OFF, MLA_Q_BLOCKS * LANE)
    kv_gain = _place(mla_kv_norm.astype(F32)[None, :], 1, MLA_LANE_OFF, MLA_KV_BLOCKS * LANE)
    qf, kf, vc = _mla_prep(front, q_gain, kv_gain, w1, wk, wv, cos_t, sin_t, b, t)
    o_c = _mla_attention(qf, kf, vc, back, b, t)

    o_d = _band_attention(back, _band_bias_tiles(rel_bias), b, t)

    outs = _out_proj_ln(o_a, o_b, o_c, o_d, w_out.astype(BF16), x2d,
                        ln_gain.astype(F32)[None, :], ln_bias.astype(F32)[None, :], layer_idx + 1 < DEPTH)
    return outs[0], (outs[1] if len(outs) > 1 else None)


def kernel(x, w_in, diff_lambda, diff_norm, gdn_conv, gdn_a_log, gdn_dt_bias, gdn_norm, mla_q_norm, mla_w_uq,
           mla_kv_norm, mla_w_ukv, rel_bias, w_out, ln_gain, ln_bias):
    b, t, d = x.shape
    assert d == D_MODEL and t % GDN_GROUP == 0 and (b * t) % IN_PROJ_ROWS == 0
    tables = _rope_tables(t)
    x2d = x.reshape(b * t, d)
    x16 = x2d.astype(BF16)
    w_t = jnp.swapaxes(w_in, 1, 2)
    for l in range(DEPTH):
        x2d, x16 = _layer(x2d, x16, l, b, t, w_t, diff_lambda[l], diff_norm[l], gdn_conv[l], gdn_a_log[l],
                          gdn_dt_bias[l], gdn_norm[l], mla_q_norm[l], mla_w_uq[l], mla_kv_norm[l],
                          mla_w_ukv[l], rel_bias[l], w_out[l], ln_gain[l], ln_bias[l], tables)
    return x2d.reshape(b, t, d)
```

```python
import functools
import math

import jax
import jax.numpy as jnp
from jax import lax
from jax.experimental import pallas as pl
from jax.experimental.pallas import tpu as pltpu

F32 = jnp.float32
BF16 = jnp.bfloat16

D_MODEL = 4096
DEPTH = 2
CHUNK = 64
N_HEADS = 8
HEAD_DIM = 128
GROUP_WIDTH = N_HEADS * HEAD_DIM
DIFF_HALF = HEAD_DIM // 2
GDN_CONV = 4
MLA_Q_RANK = 768
MLA_KV_RANK = 256
MLA_NOPE = 128
MLA_ROPE = 64
ROPE_THETA = 10000.0
BAND_CHUNKS = 9
REL_CLIP = 128
DEEPNORM_ALPHA = (2 * DEPTH) ** 0.25
IN_SIZES = (GROUP_WIDTH, GROUP_WIDTH, GROUP_WIDTH, 3 * GROUP_WIDTH, N_HEADS, N_HEADS,
            MLA_Q_RANK, MLA_KV_RANK + MLA_ROPE, GROUP_WIDTH, GROUP_WIDTH, GROUP_WIDTH, D_MODEL)

LANE = 128
IN_COLS = sum(IN_SIZES)
(OFF_DIFF_Q, OFF_DIFF_K, OFF_DIFF_V, OFF_GDN_QKV, OFF_GDN_A, OFF_GDN_B, OFF_MLA_Q, OFF_MLA_KV,
 OFF_BAND_Q, OFF_BAND_K, OFF_BAND_V, OFF_GATE) = (sum(IN_SIZES[:n]) for n in range(len(IN_SIZES)))

FRONT_TILE = 768
FRONT_COLS = 10 * FRONT_TILE
BACK_TILE = 512
BACK_SHIFT = OFF_BAND_Q % LANE
BACK_COLS = IN_COLS - OFF_BAND_Q
NEXT_ROWS = LANE
MLA_LANE_OFF = OFF_MLA_Q % LANE
assert OFF_GDN_QKV % LANE == 0 and OFF_GDN_A % LANE == 0 and OFF_GDN_B == OFF_GDN_A + N_HEADS
assert OFF_MLA_KV % LANE == MLA_LANE_OFF and OFF_GATE % LANE == BACK_SHIFT and BACK_COLS % BACK_TILE == 0
assert FRONT_COLS >= OFF_BAND_Q and (OFF_BAND_Q - BACK_SHIFT) % BACK_TILE == 0
COL_DIFF_Q = OFF_DIFF_Q // LANE
COL_DIFF_K = OFF_DIFF_K // LANE
COL_DIFF_V = OFF_DIFF_V // LANE
COL_GDN_QKV = OFF_GDN_QKV // LANE
COL_GDN_AB = OFF_GDN_A // LANE
COL_MLA = OFF_MLA_Q // LANE
COL_BAND_Q = 0
COL_BAND_K = (OFF_BAND_K - OFF_BAND_Q) // LANE
COL_BAND_V = (OFF_BAND_V - OFF_BAND_Q) // LANE
COL_GATE = (OFF_GATE - OFF_BAND_Q) // LANE
MLA_Q_BLOCKS = MLA_Q_RANK // LANE + 1
MLA_KV_BLOCKS = 3

ATT_BLOCK = 256
GDN_GROUP = 256
NEG_BIG = -1e30
VMEM_LIMIT = 48 * 1024 * 1024
IN_PROJ_ROWS = 512
BACK_VMEM_LIMIT = (2 * (BACK_TILE + NEXT_ROWS) * D_MODEL * 4 + BACK_TILE * D_MODEL * 2
                   + 2 * IN_PROJ_ROWS * D_MODEL * 2 + IN_PROJ_ROWS * BACK_TILE * (2 * 2 + 4) + (4 << 20))


def _params(*sem, vmem=VMEM_LIMIT):
    return pltpu.CompilerParams(dimension_semantics=sem, vmem_limit_bytes=vmem)


def _dot(a, b):
    return jnp.dot(a, b, preferred_element_type=F32)


def _dot_nt(a, b):
    return lax.dot_general(a, b, (((1,), (1,)), ((), ())), preferred_element_type=F32)


def _silu(x):
    return x * (1.0 / (1.0 + jnp.exp(-x)))


CAST_ROWS = 16


def _cast_rows(src_ref, src_row, dst_ref, dst_row, n_rows):
    def body(r, carry):
        src = pl.ds(pl.multiple_of(src_row + r * CAST_ROWS, CAST_ROWS), CAST_ROWS)
        dst = pl.ds(pl.multiple_of(dst_row + r * CAST_ROWS, CAST_ROWS), CAST_ROWS)
        dst_ref[dst, :] = src_ref[src, :].astype(BF16)
        return carry
    lax.fori_loop(0, n_rows // CAST_ROWS, body, 0)


def _in_proj_front_kernel(x_ref, w_ref, o_ref, w16_ref):
    @pl.when(pl.program_id(1) == 0)
    def _():
        _cast_rows(w_ref, 0, w16_ref, 0, w_ref.shape[0])

    o_ref[...] = _dot_nt(x_ref[...], w16_ref[...]).astype(o_ref.dtype)


def _in_proj_back_kernel(x_ref, w_ref, w_next_ref, o_ref, w16_ref):
    @pl.when(pl.program_id(1) == 0)
    def _():
        own = w_ref.shape[0] - BACK_SHIFT
        _cast_rows(w_ref, BACK_SHIFT, w16_ref, 0, own)
        _cast_rows(w_next_ref, 0, w16_ref, own, BACK_SHIFT)

    o_ref[...] = _dot_nt(x_ref[...], w16_ref[...]).astype(o_ref.dtype)


def _in_proj(xb, w_t, layer):
    m = xb.shape[0]
    tm = IN_PROJ_ROWS
    x_spec = pl.BlockSpec((tm, D_MODEL), lambda j, i: (i, 0))
    front = pl.pallas_call(
        _in_proj_front_kernel,
        grid=(FRONT_COLS // FRONT_TILE, m // tm),
        in_specs=[x_spec, pl.BlockSpec((None, FRONT_TILE, D_MODEL), lambda j, i: (layer, j, 0))],
        out_specs=pl.BlockSpec((tm, FRONT_TILE), lambda j, i: (i, j)),
        out_shape=jax.ShapeDtypeStruct((m, FRONT_COLS), BF16),
        scratch_shapes=[pltpu.VMEM((FRONT_TILE, D_MODEL), BF16)],
        compiler_params=_params("arbitrary", "arbitrary"),
        name="in_proj_front",
    )(xb, w_t)
    first_tile = (OFF_BAND_Q - BACK_SHIFT) // BACK_TILE
    next_blocks = BACK_TILE // NEXT_ROWS
    back = pl.pallas_call(
        _in_proj_back_kernel,
        grid=(BACK_COLS // BACK_TILE, m // tm),
        in_specs=[x_spec,
                  pl.BlockSpec((None, BACK_TILE, D_MODEL), lambda j, i: (layer, first_tile + j, 0)),
                  pl.BlockSpec((None, NEXT_ROWS, D_MODEL), lambda j, i: (layer, (first_tile + j + 1) * next_blocks, 0))],
        out_specs=pl.BlockSpec((tm, BACK_TILE), lambda j, i: (i, j)),
        out_shape=jax.ShapeDtypeStruct((m, BACK_COLS), BF16),
        scratch_shapes=[pltpu.VMEM((BACK_TILE, D_MODEL), BF16)],
        compiler_params=_params("arbitrary", "arbitrary", vmem=BACK_VMEM_LIMIT),
        name="in_proj_back",
    )(xb, w_t, w_t)
    return front, back


def _softmax_stats(m, l, s, scale, bias, shift):
    t = s if scale == 1.0 else s * scale
    if bias is not None:
        t = t + bias
    m_new = jnp.maximum(m, jnp.max(t, axis=0, keepdims=True) + shift)
    alpha = jnp.exp(m - m_new)
    p = jnp.exp(t - (m_new - shift))
    return m_new, alpha * l + jnp.sum(p, axis=0, keepdims=True), alpha, p.astype(BF16)


DIFF_HEADS_PER_STEP = 8
MLA_HEADS_PER_STEP = 8
BAND_HEADS_PER_STEP = 8


def _load_values_transposed(step, v_ref, vt_ref):
    @pl.when(step == 0)
    def _():
        for h in range(vt_ref.shape[0]):
            for j in range(vt_ref.shape[1]):
                v = v_ref[ATT_BLOCK * j:ATT_BLOCK * (j + 1), HEAD_DIM * h:HEAD_DIM * (h + 1)]
                vt_ref[h, j] = v.astype(F32).T.astype(BF16)


def _attn_sweep(qs, k_ref, vt_ref, lo, i, off_scores, diag_scores):
    heads = range(len(qs))
    nq, dk = qs[0].shape

    def keys(kb, h):
        return k_ref[pl.ds(pl.multiple_of(kb * ATT_BLOCK, ATT_BLOCK), ATT_BLOCK), dk * h:dk * (h + 1)]

    def body(kb, carry):
        s = [_dot_nt(keys(kb, h), qs[h]) for h in heads]
        prev = jnp.maximum(kb - 1, lo)
        acc = [carry[h][4] * carry[h][2] + _dot(vt_ref[h, prev], carry[h][3]) for h in heads]
        stats = [_softmax_stats(carry[h][0], carry[h][1], s[h], *off_scores(kb, h)) for h in heads]
        return tuple((stats[h][0], stats[h][1], acc[h], stats[h][3], stats[h][2]) for h in heads)

    init = tuple((jnp.full((1, nq), NEG_BIG, F32), jnp.zeros((1, nq), F32), jnp.zeros((HEAD_DIM, nq), F32),
                  jnp.zeros((ATT_BLOCK, nq), BF16), jnp.ones((1, nq), F32)) for h in heads)
    carry = lax.fori_loop(lo, i, body, init)
    s = [_dot_nt(keys(i, h), qs[h]) for h in heads]
    prev = jnp.maximum(i - 1, lo)
    acc = [carry[h][4] * carry[h][2] + _dot(vt_ref[h, prev], carry[h][3]) for h in heads]
    stats = [_softmax_stats(carry[h][0], carry[h][1], s[h], *diag_scores(h), 0.0) for h in heads]
    return [(stats[h][2] * acc[h] + _dot(vt_ref[h, i], stats[h][3]), stats[h][1]) for h in heads]


def _block_iotas():
    key = lax.broadcasted_iota(jnp.int32, (ATT_BLOCK, ATT_BLOCK), 0)
    qry = lax.broadcasted_iota(jnp.int32, (ATT_BLOCK, ATT_BLOCK), 1)
    return key, qry


def _diff_attn_kernel(lam_init, q_ref, k_ref, v_ref, gate_ref, slope_ref, lam_ref, nw_ref, o_ref, vt_ref):
    i = pl.program_id(2)
    heads = range(DIFF_HEADS_PER_STEP)
    cols = [slice(HEAD_DIM * h, HEAD_DIM * (h + 1)) for h in heads]
    key, qry = _block_iotas()
    dist = (qry - key).astype(F32)
    visible = (key // CHUNK) <= (qry // CHUNK)
    qs, slopes, bias_off, bias_diag = [], [], [], []
    for h in heads:
        q = q_ref[:, cols[h]] * (DIFF_HALF ** -0.5)
        lane = lax.broadcasted_iota(jnp.int32, q.shape, 1)
        qs.append(jnp.concatenate([jnp.where(lane < DIFF_HALF, q, jnp.zeros_like(q)),
                                   jnp.where(lane >= DIFF_HALF, q, jnp.zeros_like(q))], axis=0))
        slope = slope_ref[h][:, 0:1]
        slopes.append(slope)
        off = -slope * dist
        diag = jnp.where(visible, -slope * jnp.abs(dist), NEG_BIG)
        bias_off.append(jnp.concatenate([off, off], axis=1))
        bias_diag.append(jnp.concatenate([diag, diag], axis=1))

    def off_scores(kb, h):
        return 1.0, bias_off[h], -slopes[h] * ((i - kb) * ATT_BLOCK).astype(F32)

    def diag_scores(h):
        return 1.0, bias_diag[h]

    _load_values_transposed(i, v_ref, vt_ref)
    outs = _attn_sweep(qs, k_ref, vt_ref, 0, i, off_scores, diag_scores)
    lp = lam_ref[...]
    lam = (jnp.exp(jnp.sum(lp[0:1] * lp[1:2], axis=-1, keepdims=True))
           - jnp.exp(jnp.sum(lp[2:3] * lp[3:4], axis=-1, keepdims=True)) + lam_init)
    for h in heads:
        acc, l = outs[h]
        o = acc / l
        o = (o[:, :ATT_BLOCK] - lam * o[:, ATT_BLOCK:]).T
        o = o * lax.rsqrt(jnp.mean(o * o, axis=-1, keepdims=True) + 1e-6) * nw_ref[...]
        o = o * (1.0 - lam_init)
        o_ref[:, cols[h]] = (o * _silu(gate_ref[:, cols[h]].astype(F32))).astype(o_ref.dtype)


def _diff_attention(front, back, slopes, lam_p, norm_w, lam_init, b, t):
    nq = t // ATT_BLOCK
    m = b * t
    hp = DIFF_HEADS_PER_STEP
    assert COL_DIFF_Q % hp == 0 and COL_DIFF_K % hp == 0 and COL_DIFF_V % hp == 0 and COL_GATE % hp == 0
    blk = (ATT_BLOCK, hp * HEAD_DIM)
    return pl.pallas_call(
        functools.partial(_diff_attn_kernel, lam_init),
        grid=(b, N_HEADS // hp, nq),
        in_specs=[pl.BlockSpec(blk, lambda bi, h, i: (bi * nq + i, COL_DIFF_Q // hp + h)),
                  pl.BlockSpec((t, hp * HEAD_DIM), lambda bi, h, i: (bi, COL_DIFF_K // hp + h)),
                  pl.BlockSpec((t, hp * HEAD_DIM), lambda bi, h, i: (bi, COL_DIFF_V // hp + h)),
                  pl.BlockSpec(blk, lambda bi, h, i: (bi * nq + i, COL_GATE // hp + h)),
                  pl.BlockSpec((hp, 1, LANE), lambda bi, h, i: (h, 0, 0)),
                  pl.BlockSpec((4, LANE), lambda bi, h, i: (0, 0)),
                  pl.BlockSpec((1, HEAD_DIM), lambda bi, h, i: (0, 0))],
        out_specs=pl.BlockSpec(blk, lambda bi, h, i: (bi * nq + i, h)),
        out_shape=jax.ShapeDtypeStruct((m, GROUP_WIDTH), BF16),
        scratch_shapes=[_values_scratch(t, hp)],
        compiler_params=_params("arbitrary", "arbitrary", "arbitrary"),
        name="diff_attention",
    )(front, front, front, back, slopes, lam_p, norm_w)


def _mla_attn_kernel(q_ref, k_ref, v_ref, gate_ref, o_ref, vt_ref):
    i = pl.program_id(2)
    key, qry = _block_iotas()
    mask_diag = jnp.where((key // CHUNK) <= (qry // CHUNK), 0.0, NEG_BIG)
    scale = (MLA_NOPE + MLA_ROPE) ** -0.5

    def off_scores(kb, h):
        return scale, None, 0.0

    def diag_scores(h):
        return scale, mask_diag

    _load_values_transposed(i, v_ref, vt_ref)
    dk = 2 * LANE
    qs = [q_ref[:, dk * h:dk * (h + 1)] for h in range(MLA_HEADS_PER_STEP)]
    outs = _attn_sweep(qs, k_ref, vt_ref, 0, i, off_scores, diag_scores)
    _store_gated(outs, gate_ref, o_ref)


def _store_gated(outs, gate_ref, o_ref):
    for h, (acc, l) in enumerate(outs):
        cols = slice(HEAD_DIM * h, HEAD_DIM * (h + 1))
        o_ref[:, cols] = ((acc / l).T * _silu(gate_ref[:, cols].astype(F32))).astype(o_ref.dtype)


def _mla_attention(qf, kf, vv, back, b, t):
    nq = t // ATT_BLOCK
    m = b * t
    hp = MLA_HEADS_PER_STEP
    assert (COL_GATE + 2 * N_HEADS) % hp == 0
    return pl.pallas_call(
        _mla_attn_kernel,
        grid=(b, N_HEADS // hp, nq),
        in_specs=[pl.BlockSpec((ATT_BLOCK, hp * 2 * LANE), lambda bi, h, i: (bi * nq + i, h)),
                  pl.BlockSpec((t, hp * 2 * LANE), lambda bi, h, i: (bi, h)),
                  pl.BlockSpec((t, hp * HEAD_DIM), lambda bi, h, i: (bi, h)),
                  pl.BlockSpec((ATT_BLOCK, hp * HEAD_DIM),
                               lambda bi, h, i: (bi * nq + i, (COL_GATE + 2 * N_HEADS) // hp + h))],
        out_specs=pl.BlockSpec((ATT_BLOCK, hp * HEAD_DIM), lambda bi, h, i: (bi * nq + i, h)),
        out_shape=jax.ShapeDtypeStruct((m, GROUP_WIDTH), BF16),
        scratch_shapes=[_values_scratch(t, hp)],
        compiler_params=_params("arbitrary", "arbitrary", "arbitrary"),
        name="mla_attention",
    )(qf, kf, vv, back)


BAND_KEY_BLOCKS = (BAND_CHUNKS - 1) * CHUNK // ATT_BLOCK + 1


def _values_scratch(t, heads):
    return pltpu.VMEM((heads, t // ATT_BLOCK, HEAD_DIM, ATT_BLOCK), BF16)


def _band_attn_kernel(q_ref, k_ref, v_ref, gate_ref, bias_ref, o_ref, vt_ref):
    i = pl.program_id(2)
    scale = HEAD_DIM ** -0.5

    def off_scores(kb, h):
        return scale, bias_ref[h, kb - i + (BAND_KEY_BLOCKS - 1)], 0.0

    def diag_scores(h):
        return scale, bias_ref[h, BAND_KEY_BLOCKS - 1]

    lo = jnp.maximum(i - (BAND_KEY_BLOCKS - 1), 0)
    _load_values_transposed(i, v_ref, vt_ref)
    qs = [q_ref[:, HEAD_DIM * h:HEAD_DIM * (h + 1)] for h in range(BAND_HEADS_PER_STEP)]
    outs = _attn_sweep(qs, k_ref, vt_ref, lo, i, off_scores, diag_scores)
    _store_gated(outs, gate_ref, o_ref)


def _band_attention(back, bias_tiles, b, t):
    nq = t // ATT_BLOCK
    m = b * t
    hp = BAND_HEADS_PER_STEP
    assert COL_BAND_Q % hp == 0 and COL_BAND_K % hp == 0 and COL_BAND_V % hp == 0
    assert (COL_GATE + 3 * N_HEADS) % hp == 0
    blk = (ATT_BLOCK, hp * HEAD_DIM)
    return pl.pallas_call(
        _band_attn_kernel,
        grid=(b, N_HEADS // hp, nq),
        in_specs=[pl.BlockSpec(blk, lambda bi, h, i: (bi * nq + i, COL_BAND_Q // hp + h)),
                  pl.BlockSpec((t, hp * HEAD_DIM), lambda bi, h, i: (bi, COL_BAND_K // hp + h)),
                  pl.BlockSpec((t, hp * HEAD_DIM), lambda bi, h, i: (bi, COL_BAND_V // hp + h)),
                  pl.BlockSpec(blk, lambda bi, h, i: (bi * nq + i, (COL_GATE + 3 * N_HEADS) // hp + h)),
                  pl.BlockSpec((hp, BAND_KEY_BLOCKS, ATT_BLOCK, ATT_BLOCK), lambda bi, h, i: (h, 0, 0, 0))],
        out_specs=pl.BlockSpec(blk, lambda bi, h, i: (bi * nq + i, h)),
        out_shape=jax.ShapeDtypeStruct((m, GROUP_WIDTH), BF16),
        scratch_shapes=[_values_scratch(t, hp)],
        compiler_params=_params("arbitrary", "arbitrary", "arbitrary"),
        name="band_attention",
    )(back, back, back, back, bias_tiles)


def _band_bias_tiles(rel_bias):
    n = ATT_BLOCK
    back = (BAND_KEY_BLOCKS - 1) - jnp.arange(BAND_KEY_BLOCKS)
    j = jnp.arange(2 * n)
    q_minus_k = jnp.where(j < n, j, j - 2 * n)
    rel = back[:, None] * n + q_minus_k[None, :]
    vals = rel_bias.astype(F32)[:, jnp.clip(rel, -REL_CLIP, REL_CLIP) + REL_CLIP]
    flat = jnp.tile(vals, (1, 1, n))[:, :, :n * (2 * n - 1)]
    tiles = flat.reshape(N_HEADS, BAND_KEY_BLOCKS, n, 2 * n - 1)[:, :, :, :n]
    key = jnp.arange(n)[None, :, None]
    qry = jnp.arange(n)[None, None, :]
    chunk_back = back[:, None, None] * (n // CHUNK) + qry // CHUNK - key // CHUNK
    visible = (chunk_back >= 0) & (chunk_back < BAND_CHUNKS)
    return jnp.where(visible[None], tiles, NEG_BIG)


def _rope_block(x, ct, st):
    half = MLA_ROPE // 2
    lane = lax.broadcasted_iota(jnp.int32, (1, LANE), 1)
    rot = jnp.where(lane < MLA_LANE_OFF + half, -pltpu.roll(x, LANE - half, axis=1), pltpu.roll(x, half, axis=1))
    return x * ct + rot * st


def _mla_prep_kernel(cdq_ref, ckv_ref, qg_ref, kvg_ref, w1_ref, wk_ref, wv_ref, ct_ref, st_ref,
                     q_out, k_out, v_out):
    ct = ct_ref[...]
    st = st_ref[...]
    ckv = ckv_ref[...].astype(F32)
    cq = jnp.concatenate([cdq_ref[...].astype(F32), ckv[:, :LANE]], axis=1)
    lane_q = lax.broadcasted_iota(jnp.int32, (1, cq.shape[1]), 1)
    cq = jnp.where((lane_q >= MLA_LANE_OFF) & (lane_q < MLA_LANE_OFF + MLA_Q_RANK), cq, 0.0)
    ms_q = jnp.sum(cq * cq, axis=-1, keepdims=True) * (1.0 / MLA_Q_RANK)
    nq = (cq * lax.rsqrt(ms_q + 1e-6) * qg_ref[...]).astype(BF16)
    q1 = _dot(nq, w1_ref[...])
    lane_kv = lax.broadcasted_iota(jnp.int32, (1, ckv.shape[1]), 1)
    lat = jnp.where((lane_kv >= MLA_LANE_OFF) & (lane_kv < MLA_LANE_OFF + MLA_KV_RANK), ckv, 0.0)
    ms_kv = jnp.sum(lat * lat, axis=-1, keepdims=True) * (1.0 / MLA_KV_RANK)
    nkv = (lat * lax.rsqrt(ms_kv + 1e-6) * kvg_ref[...]).astype(BF16)
    kn = _dot(nkv, wk_ref[...])
    v_out[...] = _dot(nkv, wv_ref[...]).astype(v_out.dtype)
    kr = _rope_block(ckv[:, 2 * LANE:], ct, st).astype(k_out.dtype)
    for h in range(N_HEADS):
        a, bq = 2 * LANE * h, LANE * h
        q_out[:, a:a + LANE] = q1[:, a:a + LANE].astype(q_out.dtype)
        q_out[:, a + LANE:a + 2 * LANE] = _rope_block(q1[:, a + LANE:a + 2 * LANE], ct, st).astype(q_out.dtype)
        k_out[:, a:a + LANE] = kn[:, bq:bq + LANE].astype(k_out.dtype)
        k_out[:, a + LANE:a + 2 * LANE] = kr


def _mla_prep(front, q_gain, kv_gain, w1, wk, wv, cos_t, sin_t, b, t):
    m = b * t
    tm = 256
    nt = t // tm
    full = lambda shape: pl.BlockSpec(shape, lambda i: (0, 0))
    q_blocks = MLA_Q_BLOCKS - 1
    assert COL_MLA % q_blocks == 0 and (COL_MLA + q_blocks) % MLA_KV_BLOCKS == 0
    return pl.pallas_call(
        _mla_prep_kernel,
        grid=(m // tm,),
        in_specs=[pl.BlockSpec((tm, q_blocks * LANE), lambda i: (i, COL_MLA // q_blocks)),
                  pl.BlockSpec((tm, MLA_KV_BLOCKS * LANE), lambda i: (i, (COL_MLA + q_blocks) // MLA_KV_BLOCKS)),
                  full(q_gain.shape), full(kv_gain.shape),
                  full(w1.shape), full(wk.shape), full(wv.shape),
                  pl.BlockSpec((tm, LANE), lambda i: (i % nt, 0)),
                  pl.BlockSpec((tm, LANE), lambda i: (i % nt, 0))],
        out_specs=[pl.BlockSpec((tm, 2 * GROUP_WIDTH), lambda i: (i, 0)),
                   pl.BlockSpec((tm, 2 * GROUP_WIDTH), lambda i: (i, 0)),
                   pl.BlockSpec((tm, GROUP_WIDTH), lambda i: (i, 0))],
        out_shape=[jax.ShapeDtypeStruct((m, 2 * GROUP_WIDTH), BF16),
                   jax.ShapeDtypeStruct((m, 2 * GROUP_WIDTH), BF16),
                   jax.ShapeDtypeStruct((m, GROUP_WIDTH), BF16)],
        compiler_params=_params("arbitrary"),
        name="mla_prep",
    )(front, front, q_gain, kv_gain, w1, wk, wv, cos_t, sin_t)


HALO = 16


def _gdn_prep_kernel(blocks_per_seq, x_ref, halo_ref, ab_ref, cw_ref, alog_ref, dtb_ref,
                     qn_ref, kn_ref, v_ref, gcol_ref, bcol_ref, grow_ref, cat_ref):
    i = pl.program_id(0)
    halo_scale = jnp.where(i % blocks_per_seq == 0, 0.0, 1.0)
    outs = (qn_ref, kn_ref, v_ref)
    rows = x_ref.shape[0]
    for j in range(3 * N_HEADS):
        sl = slice(LANE * j, LANE * (j + 1))
        cat_ref[j, :HALO, :] = halo_ref[:, sl].astype(F32) * halo_scale
        cat_ref[j, HALO:, :] = x_ref[:, sl].astype(F32)
        w = cw_ref[:, sl]
        acc = cat_ref[j, HALO:, :] * w[GDN_CONV - 1:GDN_CONV]
        for s in range(1, GDN_CONV):
            acc = acc + cat_ref[j, HALO - s:HALO - s + rows, :] * w[GDN_CONV - 1 - s:GDN_CONV - s]
        y = _silu(acc)
        part, head = divmod(j, N_HEADS)
        if part < 2:
            norm = lax.rsqrt(jnp.sum(y * y, axis=-1, keepdims=True) + 1e-6)
            y = y * (norm * HEAD_DIM ** -0.5 if part == 0 else norm)
        outs[part][:, LANE * head:LANE * (head + 1)] = y.astype(BF16)

    ab = ab_ref[...].astype(F32)
    z = ab + dtb_ref[...]
    softplus = jnp.maximum(z, 0.0) + jnp.log1p(jnp.exp(-jnp.abs(z)))
    g = -jnp.exp(alog_ref[...]) * softplus
    beta = 1.0 / (1.0 + jnp.exp(-ab))
    row = lax.broadcasted_iota(jnp.int32, g.shape, 0) % CHUNK
    s = 1
    while s < CHUNK:
        g = g + jnp.where(row >= s, pltpu.roll(g, s, axis=0), 0.0)
        s *= 2
    gt = g.T
    for h in range(N_HEADS):
        gcol_ref[h] = jnp.broadcast_to(g[:, h:h + 1], g.shape)
        bcol_ref[h] = jnp.broadcast_to(beta[:, N_HEADS + h:N_HEADS + h + 1], g.shape)
        grow_ref[h] = gt[h:h + 1, :]


def _gdn_prep(front, conv_w, alog, dtb, b, t):
    m = b * t
    tm = GDN_GROUP
    nt = t // tm
    assert OFF_GDN_QKV % (3 * GROUP_WIDTH) == 0
    qkv_block = OFF_GDN_QKV // (3 * GROUP_WIDTH)
    return pl.pallas_call(
        functools.partial(_gdn_prep_kernel, nt),
        grid=(m // tm,),
        in_specs=[pl.BlockSpec((tm, 3 * GROUP_WIDTH), lambda i: (i, qkv_block)),
                  pl.BlockSpec((HALO, 3 * GROUP_WIDTH), lambda i: (jnp.maximum(i * (tm // HALO) - 1, 0), qkv_block)),
                  pl.BlockSpec((tm, LANE), lambda i: (i, COL_GDN_AB)),
                  pl.BlockSpec((GDN_CONV, 3 * GROUP_WIDTH), lambda i: (0, 0)),
                  pl.BlockSpec((1, LANE), lambda i: (0, 0)),
                  pl.BlockSpec((1, LANE), lambda i: (0, 0))],
        out_specs=[pl.BlockSpec((tm, GROUP_WIDTH), lambda i: (i, 0)),
                   pl.BlockSpec((tm, GROUP_WIDTH), lambda i: (i, 0)),
                   pl.BlockSpec((tm, GROUP_WIDTH), lambda i: (i, 0)),
                   pl.BlockSpec((None, N_HEADS, tm, LANE), lambda i: (i // nt, 0, i % nt, 0)),
                   pl.BlockSpec((None, N_HEADS, tm, LANE), lambda i: (i // nt, 0, i % nt, 0)),
                   pl.BlockSpec((None, N_HEADS, 1, tm), lambda i: (i // nt, 0, 0, i % nt))],
        out_shape=[jax.ShapeDtypeStruct((m, GROUP_WIDTH), BF16),
                   jax.ShapeDtypeStruct((m, GROUP_WIDTH), BF16),
                   jax.ShapeDtypeStruct((m, GROUP_WIDTH), BF16),
                   jax.ShapeDtypeStruct((b, N_HEADS, t, LANE), F32),
                   jax.ShapeDtypeStruct((b, N_HEADS, t, LANE), F32),
                   jax.ShapeDtypeStruct((b, N_HEADS, 1, t), F32)],
        scratch_shapes=[pltpu.VMEM((3 * N_HEADS, HALO + tm, LANE), F32)],
        compiler_params=_params("arbitrary"),
        name="gdn_prep",
    )(front, front, front, conv_w, alog, dtb)


GDN_HEADS_PER_STEP = 8


def _gdn_kernel(q_ref, k_ref, v_ref, gc_ref, bc_ref, gr_ref, gate_ref, nw_ref, o_ref, state_ref):
    n = pl.program_id(2)

    @pl.when(n == 0)
    def _():
        state_ref[...] = jnp.zeros_like(state_ref)

    g_sz = GDN_GROUP
    ri = lax.broadcasted_iota(jnp.int32, (g_sz, g_sz), 0)
    ci = lax.broadcasted_iota(jnp.int32, (g_sz, g_sz), 1)
    dif = jnp.where((ri // CHUNK) == (ci // CHUNK), ri - ci, -1)
    heads = range(GDN_HEADS_PER_STEP)
    cols = [slice(HEAD_DIM * hb, HEAD_DIM * (hb + 1)) for hb in heads]
    outs = _gdn_heads(dif, [q_ref[:, sl] for sl in cols], [k_ref[:, sl] for sl in cols],
                      [v_ref[:, sl] for sl in cols], [gc_ref[hb] for hb in heads], [bc_ref[hb] for hb in heads],
                      [gr_ref[hb] for hb in heads], state_ref)
    for hb in heads:
        o = outs[hb]
        o = o * lax.rsqrt(jnp.mean(o * o, axis=-1, keepdims=True) + 1e-6) * nw_ref[...]
        o_ref[:, cols[hb]] = (o * _silu(gate_ref[:, cols[hb]].astype(F32))).astype(o_ref.dtype)


def _gdn_heads(dif, q16, k16, v16, gc, beta, gr, state_ref):
    heads = range(len(q16))
    n_chunks = GDN_GROUP // CHUNK
    incl = dif >= 0
    kf = [k16[h].astype(F32) for h in heads]
    eg = [jnp.exp(gc[h]) for h in heads]
    kbeta = [kf[h] * beta[h] for h in heads]
    decay = [jnp.where(incl, jnp.exp(jnp.where(incl, jnp.concatenate([gc[h], gc[h]], axis=1) - gr[h], 0.0)), 0.0)
             for h in heads]

    kk = [_dot_nt(kbeta[h].astype(BF16), k16[h]) for h in heads]
    mpow = [jnp.where(dif > 0, -(kk[h] * decay[h]), 0.0) for h in heads]
    inv = [jnp.where(dif == 0, 1.0, 0.0) + mpow[h] for h in heads]
    mpow16 = [mpow[h].astype(BF16) for h in heads]
    for _ in range(int(math.log2(CHUNK)) - 1):
        mpow16 = [_dot(mpow16[h], mpow16[h]).astype(BF16) for h in heads]
        inv = [inv[h] + _dot(inv[h].astype(BF16), mpow16[h]) for h in heads]

    rhs = [jnp.concatenate([v16[h].astype(F32) * beta[h], kbeta[h] * eg[h]], axis=1).astype(BF16) for h in heads]
    uw = [_dot(inv[h].astype(BF16), rhs[h]) for h in heads]
    u = [uw[h][:, :HEAD_DIM] for h in heads]
    w = [uw[h][:, HEAD_DIM:].astype(BF16) for h in heads]
    qk = [(_dot_nt(q16[h], k16[h]) * decay[h]).astype(BF16) for h in heads]
    qe = [(q16[h].astype(F32) * eg[h]).astype(BF16) for h in heads]
    g_last_rows = [[gc[h][CHUNK * (c + 1) - 1:CHUNK * (c + 1), :] for c in range(n_chunks)] for h in heads]
    kdec_t = [(kf[h] * jnp.exp(jnp.concatenate([jnp.broadcast_to(g, (CHUNK, LANE)) for g in g_last_rows[h]], axis=0)
                               - gc[h])).T.astype(BF16) for h in heads]
    col_chunk = lax.broadcasted_iota(jnp.int32, kdec_t[0].shape, 1) // CHUNK

    vparts = [[u[h][CHUNK * c:CHUNK * (c + 1)] for c in range(n_chunks)] for h in heads]
    state = [state_ref[h] for h in heads]
    outs = [[] for _ in heads]
    for c in range(n_chunks):
        rows = slice(CHUNK * c, CHUNK * (c + 1))
        s16 = [state[h].astype(BF16) for h in heads]
        ws = [_dot(w[h][rows], s16[h]) for h in heads]
        for h in heads:
            vparts[h][c] = u[h][rows] - ws[h]
        vfull = [jnp.concatenate(vparts[h], axis=0).astype(BF16) for h in heads]
        upd = [_dot(jnp.where(col_chunk == c, kdec_t[h], jnp.zeros_like(kdec_t[h])), vfull[h]) for h in heads]
        state = [state[h] * jnp.exp(g_last_rows[h][c]) + upd[h] for h in heads]
        for h in heads:
            outs[h].append(_dot(qe[h][rows], s16[h]) + _dot(qk[h][rows], vfull[h]))
    for h in heads:
        state_ref[h] = state[h]
    return [jnp.concatenate(outs[h], axis=0) for h in heads]


def _gated_delta_net(qn, kn, vv, gcol, bcol, grow, back, norm_w, b, t):
    assert (COL_GATE + N_HEADS) % GDN_HEADS_PER_STEP == 0
    m = b * t
    ng = t // GDN_GROUP
    hb = GDN_HEADS_PER_STEP
    blk = (GDN_GROUP, HEAD_DIM * hb)
    tok = lambda bi, h, n: (bi * ng + n, h)
    return pl.pallas_call(
        _gdn_kernel,
        grid=(b, N_HEADS // hb, ng),
        in_specs=[pl.BlockSpec(blk, tok), pl.BlockSpec(blk, tok), pl.BlockSpec(blk, tok),
                  pl.BlockSpec((None, hb, GDN_GROUP, LANE), lambda bi, h, n: (bi, h, n, 0)),
                  pl.BlockSpec((None, hb, GDN_GROUP, LANE), lambda bi, h, n: (bi, h, n, 0)),
                  pl.BlockSpec((None, hb, 1, GDN_GROUP), lambda bi, h, n: (bi, h, 0, n)),
                  pl.BlockSpec(blk, lambda bi, h, n: (bi * ng + n, (COL_GATE + N_HEADS) // hb + h)),
                  pl.BlockSpec((1, HEAD_DIM), lambda bi, h, n: (0, 0))],
        out_specs=pl.BlockSpec(blk, tok),
        out_shape=jax.ShapeDtypeStruct((m, GROUP_WIDTH), BF16),
        scratch_shapes=[pltpu.VMEM((hb, HEAD_DIM, HEAD_DIM), F32)],
        compiler_params=_params("arbitrary", "arbitrary", "arbitrary"),
        name="gated_delta_net",
    )(qn, kn, vv, gcol, bcol, grow, back, norm_w)


def _out_proj_kernel(oa_ref, ob_ref, oc_ref, od_ref, w_ref, x_ref, g_ref, b_ref, y_ref, *y16_ref):
    o = jnp.concatenate([oa_ref[...], ob_ref[...], oc_ref[...], od_ref[...]], axis=1)
    z = _dot(o, w_ref[...]) + DEEPNORM_ALPHA * x_ref[...]
    zc = z - jnp.mean(z, axis=-1, keepdims=True)
    y = zc * lax.rsqrt(jnp.mean(zc * zc, axis=-1, keepdims=True) + 1e-5) * g_ref[...] + b_ref[...]
    y_ref[...] = y
    for ref in y16_ref:
        ref[...] = y.astype(BF16)


def _out_proj_ln(oa, ob, oc, od, w16, x2d, gain, bias, with_bf16):
    m = x2d.shape[0]
    tm = 128
    grp = pl.BlockSpec((tm, GROUP_WIDTH), lambda i: (i, 0))
    row = pl.BlockSpec((tm, D_MODEL), lambda i: (i, 0))
    vec = pl.BlockSpec((1, D_MODEL), lambda i: (0, 0))
    weight = pl.BlockSpec((D_MODEL, D_MODEL), lambda i: (0, 0), pipeline_mode=pl.Buffered(1))
    return pl.pallas_call(
        _out_proj_kernel,
        grid=(m // tm,),
        in_specs=[grp, grp, grp, grp, weight, row, vec, vec],
        out_specs=[row, row][:1 + with_bf16],
        out_shape=[jax.ShapeDtypeStruct((m, D_MODEL), F32),
                   jax.ShapeDtypeStruct((m, D_MODEL), BF16)][:1 + with_bf16],
        compiler_params=_params("arbitrary"),
        name="out_proj_layernorm",
    )(oa, ob, oc, od, w16, x2d, gain, bias)


def _place(a, axis, offset, size):
    pad = [(0, 0)] * a.ndim
    pad[axis] = (offset, size - offset - a.shape[axis])
    return jnp.pad(a, pad)


def _layout_w_uq(w):
    w = w.reshape(MLA_Q_RANK, N_HEADS, MLA_NOPE + MLA_ROPE)
    w1 = jnp.concatenate([w[:, :, :MLA_NOPE], _place(w[:, :, MLA_NOPE:], 2, MLA_LANE_OFF, LANE)], axis=2)
    w1 = w1.reshape(MLA_Q_RANK, N_HEADS * 2 * LANE)
    return _place(w1, 0, MLA_LANE_OFF, MLA_Q_BLOCKS * LANE).astype(BF16)


def _layout_w_ukv(w):
    w = w.reshape(MLA_KV_RANK, N_HEADS, MLA_NOPE + HEAD_DIM)
    wk = w[:, :, :MLA_NOPE].reshape(MLA_KV_RANK, GROUP_WIDTH)
    wv = w[:, :, MLA_NOPE:].reshape(MLA_KV_RANK, GROUP_WIDTH)
    rows = MLA_KV_BLOCKS * LANE
    return _place(wk, 0, MLA_LANE_OFF, rows).astype(BF16), _place(wv, 0, MLA_LANE_OFF, rows).astype(BF16)


def _rope_tables(t):
    half = MLA_ROPE // 2
    inv = ROPE_THETA ** (-jnp.arange(half, dtype=F32) / half)
    ang = jnp.arange(t).astype(F32)[:, None] * inv[None, :]
    cos_t = _place(jnp.concatenate([jnp.cos(ang), jnp.cos(ang)], axis=1), 1, MLA_LANE_OFF, LANE)
    sin_t = _place(jnp.concatenate([jnp.sin(ang), jnp.sin(ang)], axis=1), 1, MLA_LANE_OFF, LANE)
    return cos_t, sin_t


def _pad_lanes(v):
    return jnp.concatenate([v.astype(F32), jnp.zeros((LANE - v.shape[0],), F32)])[None, :]


def _layer(x2d, x16, layer_idx, b, t, w_in, diff_lambda, diff_norm, gdn_conv, gdn_a_log, gdn_dt_bias, gdn_norm,
           mla_q_norm, mla_w_uq, mla_kv_norm, mla_w_ukv, rel_bias, w_out, ln_gain, ln_bias, tables):
    front, back = _in_proj(x16, w_in, layer_idx)

    lam_init = 0.8 - 0.6 * math.exp(-0.3 * layer_idx)
    slopes = 2.0 ** (-8.0 * jnp.arange(1, N_HEADS + 1, dtype=F32) / N_HEADS)
    slopes = jnp.broadcast_to(slopes[:, None, None], (N_HEADS, 1, LANE))
    lam_p = jnp.concatenate([diff_lambda.astype(F32), jnp.zeros((4, LANE - DIFF_HALF), F32)], axis=1)
    o_a = _diff_attention(front, back, slopes, lam_p, diff_norm.astype(F32)[None, :], lam_init, b, t)

    qn, kn, vv, gcol, bcol, grow = _gdn_prep(front, gdn_conv.astype(F32), _pad_lanes(gdn_a_log),
                                              _pad_lanes(gdn_dt_bias), b, t)
    o_b = _gated_delta_net(qn, kn, vv, gcol, bcol, grow, back, gdn_norm.astype(F32)[None, :], b, t)

    w1 = _layout_w_uq(mla_w_uq)
    wk, wv = _layout_w_ukv(mla_w_ukv)
    cos_t, sin_t = tables
    q_gain = _place(mla_q_norm.astype(F32)[None, :], 1, MLA_LANE_OFF, MLA_Q_BLOCKS * LANE)
    kv_gain = _place(mla_kv_norm.astype(F32)[None, :], 1, MLA_LANE_OFF, MLA_KV_BLOCKS * LANE)
    qf, kf, vc = _mla_prep(front, q_gain, kv_gain, w1, wk, wv, cos_t, sin_t, b, t)
    o_c = _mla_attention(qf, kf, vc, back, b, t)

    o_d = _band_attention(back, _band_bias_tiles(rel_bias), b, t)

    outs = _out_proj_ln(o_a, o_b, o_c, o_d, w_out.astype(BF16), x2d,
                        ln_gain.astype(F32)[None, :], ln_bias.astype(F32)[None, :], layer_idx + 1 < DEPTH)
    return outs[0], (outs[1] if len(outs) > 1 else None)


def kernel(x, w_in, diff_lambda, diff_norm, gdn_conv, gdn_a_log, gdn_dt_bias, gdn_norm, mla_q_norm, mla_w_uq,
           mla_kv_norm, mla_w_ukv, rel_bias, w_out, ln_gain, ln_bias):
    b, t, d = x.shape
    assert d == D_MODEL and t % GDN_GROUP == 0 and (b * t) % IN_PROJ_ROWS == 0
    tables = _rope_tables(t)
    x2d = x.reshape(b * t, d)
    x16 = x2d.astype(BF16)
    w_t = jnp.swapaxes(w_in, 1, 2)
    for l in range(DEPTH):
        x2d, x16 = _layer(x2d, x16, l, b, t, w_t, diff_lambda[l], diff_norm[l], gdn_conv[l], gdn_a_log[l],
                          gdn_dt_bias[l], gdn_norm[l], mla_q_norm[l], mla_w_uq[l], mla_kv_norm[l],
                          mla_w_ukv[l], rel_bias[l], w_out[l], ln_gain[l], ln_bias[l], tables)
    return x2d.reshape(b, t, d)
```
